```python
import math
import jax
import jax.numpy as jnp
from jax import lax
import numpy as np

D_MODEL = 1024
BATCH = 1
SEQ = 16384
DEPTH = 4

N_MIXERS = 3
GLA_HEADS = 4
GLA_DK = D_MODEL // 2 // GLA_HEADS
GLA_DV = D_MODEL // GLA_HEADS
GLA_RANK = 16
GLA_NORMALIZER = 16.0
GLA_CHUNK = 64
GLA_IN = 2 * GLA_HEADS * GLA_DK + 2 * GLA_HEADS * GLA_DV + GLA_RANK
GDN_HEADS = 8
GDN_DK = D_MODEL // GDN_HEADS
GDN_DV = D_MODEL // GDN_HEADS
GDN_CONV = 4
GDN_CHUNK = 64
GDN_CONV_CH = 2 * GDN_HEADS * GDN_DK + GDN_HEADS * GDN_DV
GDN_IN = GDN_CONV_CH + GDN_HEADS * GDN_DV + 2 * GDN_HEADS
SG_WIDTH = D_MODEL
SG_GROUPS = 8
SG_CHUNK = 128
D_FF = ((8 * D_MODEL // 3 + 127) // 128) * 128
FFN_CONV = 3
LN_EPS = 1e-5
RMS_EPS = 1e-6
ALPHA = (2 * DEPTH) ** 0.25
BETA = (8 * DEPTH) ** -0.25

N_GLA_LAYERS = len(range(0, DEPTH, N_MIXERS))
N_GDN_LAYERS = len(range(1, DEPTH, N_MIXERS))
N_SGU_LAYERS = len(range(2, DEPTH, N_MIXERS))

kernel_name = "hybrid_gla_gdn_sgu_convffn_deepnorm"


def layer_norm(x, g, b):
    xf = x.astype(jnp.float32)
    mu = jnp.mean(xf, -1, keepdims=True)
    var = jnp.mean(jnp.square(xf - mu), -1, keepdims=True)
    return ((xf - mu) * lax.rsqrt(var + LN_EPS)).astype(x.dtype) * g + b


def rms_norm(x, w):
    xf = x.astype(jnp.float32)
    return xf * lax.rsqrt(jnp.mean(jnp.square(xf), -1, keepdims=True) + RMS_EPS) * w.astype(jnp.float32)


def l2_normalize(x):
    return x * lax.rsqrt(jnp.sum(jnp.square(x), -1, keepdims=True) + RMS_EPS)


def causal_dwconv(x, w):
    k, c = w.shape
    return lax.conv_general_dilated(
        x, w[:, None, :].astype(x.dtype), window_strides=(1,), padding=((k - 1, 0),),
        dimension_numbers=("NWC", "WIO", "NWC"), feature_group_count=c)


def to_head_chunks(x, n_heads, chunk):
    b, s, _ = x.shape
    return x.reshape(b, s // chunk, chunk, n_heads, -1).transpose(0, 3, 1, 2, 4)


def scalar_head_chunks(x, chunk):
    b, s, h = x.shape
    return x.reshape(b, s // chunk, chunk, h).transpose(0, 3, 1, 2)


def from_head_chunks(o):
    b, h, n, c, d = o.shape
    return o.transpose(0, 2, 3, 1, 4).reshape(b, n * c, h * d)


def gla_chunked(q, k, v, g):
    c = q.shape[-2]
    bcum = jnp.cumsum(g, axis=-2)
    b_last = bcum[..., -1:, :]
    b_ref = bcum[..., c // 2:c // 2 + 1, :]
    causal = jnp.tril(jnp.ones((c, c), dtype=bool))
    scores = jnp.einsum('bhnik,bhnjk->bhnij', q * jnp.exp(bcum - b_ref), k * jnp.exp(b_ref - bcum))
    o_intra = jnp.einsum('bhnij,bhnjv->bhniv', jnp.where(causal, scores, 0.0), v)
    d_state = jnp.einsum('bhnck,bhncv->bhnkv', k * jnp.exp(b_last - bcum), v)
    chunk_decay = jnp.exp(b_last[..., 0, :])

    def step(state, inp):
        dec, ds = inp
        return state * dec[..., None] + ds, state

    bsz, h, _, _, dk = q.shape
    s0 = jnp.zeros((bsz, h, dk, v.shape[-1]), jnp.float32)
    _, s_start = lax.scan(step, s0, (jnp.moveaxis(chunk_decay, 2, 0), jnp.moveaxis(d_state, 2, 0)))
    s_start = jnp.moveaxis(s_start, 0, 2)
    o_inter = jnp.einsum('bhnck,bhnkv->bhncv', q * jnp.exp(bcum), s_start)
    return o_intra + o_inter


def gla_mixer(x, w_in, w_gk2, b_gk, norm_w, w_out):
    hk, hv = GLA_HEADS * GLA_DK, GLA_HEADS * GLA_DV
    proj = x @ w_in
    q, k, v, gate, gk_lr = jnp.split(proj, [hk, 2 * hk, 2 * hk + hv, 2 * hk + 2 * hv], axis=-1)
    gk = jax.nn.log_sigmoid((gk_lr @ w_gk2 + b_gk).astype(jnp.float32)) / GLA_NORMALIZER
    f32 = jnp.float32
    qc = to_head_chunks(q.astype(f32) * GLA_DK ** -0.5, GLA_HEADS, GLA_CHUNK)
    kc = to_head_chunks(k.astype(f32), GLA_HEADS, GLA_CHUNK)
    vc = to_head_chunks(v.astype(f32), GLA_HEADS, GLA_CHUNK)
    gc = to_head_chunks(gk, GLA_HEADS, GLA_CHUNK)
    o = rms_norm(gla_chunked(qc, kc, vc, gc), norm_w)
    o = from_head_chunks(o).astype(x.dtype)
    return (o * jax.nn.silu(gate)) @ w_out


def gdn_chunked(q, k, v, beta, g):
    c = q.shape[-2]
    bcum = jnp.cumsum(g, axis=-1)
    incl = jnp.tril(jnp.ones((c, c), dtype=bool))
    strict = jnp.tril(jnp.ones((c, c), dtype=bool), -1)
    diff = bcum[..., :, None] - bcum[..., None, :]
    decay_ij = jnp.exp(jnp.where(incl, diff, -jnp.inf))
    k_beta = k * beta[..., None]
    m = jnp.where(strict, jnp.einsum('bhnik,bhnjk->bhnij', k_beta, k) * decay_ij, 0.0)
    rhs = jnp.concatenate([v * beta[..., None], k_beta * jnp.exp(bcum)[..., None]], axis=-1)
    sol = lax.linalg.triangular_solve(m, rhs, left_side=True, lower=True, unit_diagonal=True)
    u, w = jnp.split(sol, [v.shape[-1]], axis=-1)
    qk = jnp.einsum('bhnik,bhnjk->bhnij', q, k) * decay_ij
    q_dec = q * jnp.exp(bcum)[..., None]
    k_dec = k * jnp.exp(bcum[..., -1:] - bcum)[..., None]
    chunk_decay = jnp.exp(bcum[..., -1])

    def step(state, inp):
        qk_n, u_n, w_n, qd_n, kd_n, cd_n = inp
        v_new = u_n - jnp.einsum('bhck,bhkv->bhcv', w_n, state)
        o_n = jnp.einsum('bhck,bhkv->bhcv', qd_n, state) + jnp.einsum('bhij,bhjv->bhiv', qk_n, v_new)
        state = state * cd_n[..., None, None] + jnp.einsum('bhck,bhcv->bhkv', kd_n, v_new)
        return state, o_n

    bsz, h, _, _, dk = q.shape
    s0 = jnp.zeros((bsz, h, dk, v.shape[-1]), jnp.float32)
    xs = tuple(jnp.moveaxis(a, 2, 0) for a in (qk, u, w, q_dec, k_dec, chunk_decay))
    _, o = lax.scan(step, s0, xs)
    return jnp.moveaxis(o, 0, 2)


def gdn_mixer(x, w_in, conv_w, a_log, dt_bias, norm_w, w_out):
    hk, hv = GDN_HEADS * GDN_DK, GDN_HEADS * GDN_DV
    proj = x @ w_in
    qkv, gate, beta_lin, a_lin = jnp.split(
        proj, [GDN_CONV_CH, GDN_CONV_CH + hv, GDN_CONV_CH + hv + GDN_HEADS], axis=-1)
    qkv = jax.nn.silu(causal_dwconv(qkv, conv_w))
    q, k, v = jnp.split(qkv.astype(jnp.float32), [hk, 2 * hk], axis=-1)
    qc = l2_normalize(to_head_chunks(q, GDN_HEADS, GDN_CHUNK)) * GDN_DK ** -0.5
    kc = l2_normalize(to_head_chunks(k, GDN_HEADS, GDN_CHUNK))
    vc = to_head_chunks(v, GDN_HEADS, GDN_CHUNK)
    beta = scalar_head_chunks(jax.nn.sigmoid(beta_lin.astype(jnp.float32)), GDN_CHUNK)
    g = -jnp.exp(a_log.astype(jnp.float32)) * jax.nn.softplus(
        a_lin.astype(jnp.float32) + dt_bias.astype(jnp.float32))
    gc = scalar_head_chunks(g, GDN_CHUNK)
    o = rms_norm(gdn_chunked(qc, kc, vc, beta, gc), norm_w)
    o = from_head_chunks(o).astype(x.dtype)
    return (o * jax.nn.silu(gate)) @ w_out


def sgu_mixer(x, w_in, ln_g, ln_b, w_sp, b_sp, w_out):
    z = jax.nn.gelu(x @ w_in, approximate=False)
    u, v = jnp.split(z, 2, axis=-1)
    v = layer_norm(v, ln_g, ln_b)
    bsz, s, _ = v.shape
    v = v.reshape(bsz, s // SG_CHUNK, SG_CHUNK, SG_GROUPS, SG_WIDTH // SG_GROUPS)
    causal = jnp.tril(jnp.ones((SG_CHUNK, SG_CHUNK), dtype=bool))
    w_causal = jnp.where(causal, w_sp, 0.0).astype(v.dtype)
    mixed = jnp.einsum('gts,bnsgd->bntgd', w_causal, v) + b_sp.T[:, :, None]
    return (u * mixed.reshape(bsz, s, SG_WIDTH)) @ w_out


def conv_ffn(x, w_in, conv_w, w_out):
    h = causal_dwconv(x @ w_in, conv_w)
    gate, up = jnp.split(h, 2, axis=-1)
    return (jax.nn.gelu(gate, approximate=False) * up) @ w_out


def setup_inputs(seed: int = 0) -> dict:
    key = jax.random.key(seed)
    ks = jax.random.split(key, 24)
    f32 = jnp.float32

    def nrm(k, shape, scale):
        return jax.random.normal(k, shape, f32) * scale

    na, nb, nc = N_GLA_LAYERS, N_GDN_LAYERS, N_SGU_LAYERS
    x = nrm(ks[0], (BATCH, SEQ, D_MODEL), 1.0)
    gla_w_in = nrm(ks[1], (na, D_MODEL, GLA_IN), D_MODEL ** -0.5)
    gla_w_gk2 = nrm(ks[2], (na, GLA_RANK, GLA_HEADS * GLA_DK), GLA_RANK ** -0.5)
    gla_b_gk = nrm(ks[3], (na, GLA_HEADS * GLA_DK), 0.01)
    gla_norm_w = 1.0 + nrm(ks[4], (na, GLA_DV), 0.01)
    gla_w_out = nrm(ks[5], (na, GLA_HEADS * GLA_DV, D_MODEL), (GLA_HEADS * GLA_DV) ** -0.5 * BETA)
    gdn_w_in = nrm(ks[6], (nb, D_MODEL, GDN_IN), D_MODEL ** -0.5)
    gdn_conv_w = nrm(ks[7], (nb, GDN_CONV, GDN_CONV_CH), GDN_CONV ** -0.5)
    gdn_a_log = jnp.log(jax.random.uniform(ks[8], (nb, GDN_HEADS), f32, 1.0, 16.0))
    dt = jnp.exp(jax.random.uniform(ks[9], (nb, GDN_HEADS), f32, math.log(1e-3), math.log(1e-1)))
    gdn_dt_bias = dt + jnp.log(-jnp.expm1(-dt))
    gdn_norm_w = 1.0 + nrm(ks[10], (nb, GDN_DV), 0.01)
    gdn_w_out = nrm(ks[11], (nb, GDN_HEADS * GDN_DV, D_MODEL), (GDN_HEADS * GDN_DV) ** -0.5 * BETA)
    sg_w_in = nrm(ks[12], (nc, D_MODEL, 2 * SG_WIDTH), D_MODEL ** -0.5)
    sg_ln_g = 1.0 + nrm(ks[13], (nc, SG_WIDTH), 0.01)
    sg_ln_b = nrm(ks[14], (nc, SG_WIDTH), 0.01)
    sg_w_sp = nrm(ks[15], (nc, SG_GROUPS, SG_CHUNK, SG_CHUNK), 0.5 * SG_CHUNK ** -0.5)
    sg_b_sp = 1.0 + nrm(ks[16], (nc, SG_GROUPS, SG_CHUNK), 0.01)
    sg_w_out = nrm(ks[17], (nc, SG_WIDTH, D_MODEL), SG_WIDTH ** -0.5 * BETA)
    ffn_w_in = nrm(ks[18], (DEPTH, D_MODEL, 2 * D_FF), D_MODEL ** -0.5)
    ffn_conv_w = nrm(ks[19], (DEPTH, FFN_CONV, 2 * D_FF), FFN_CONV ** -0.5)
    ffn_w_out = nrm(ks[20], (DEPTH, D_FF, D_MODEL), D_FF ** -0.5 * BETA)
    ln_g = 1.0 + nrm(ks[21], (DEPTH, 2, D_MODEL), 0.01)
    ln_b = nrm(ks[22], (DEPTH, 2, D_MODEL), 0.01)
    return {"x": x,
            "gla_w_in": gla_w_in, "gla_w_gk2": gla_w_gk2, "gla_b_gk": gla_b_gk,
            "gla_norm_w": gla_norm_w, "gla_w_out": gla_w_out,
            "gdn_w_in": gdn_w_in, "gdn_conv_w": gdn_conv_w, "gdn_a_log": gdn_a_log,
            "gdn_dt_bias": gdn_dt_bias, "gdn_norm_w": gdn_norm_w, "gdn_w_out": gdn_w_out,
            "sg_w_in": sg_w_in, "sg_ln_g": sg_ln_g, "sg_ln_b": sg_ln_b,
            "sg_w_sp": sg_w_sp, "sg_b_sp": sg_b_sp, "sg_w_out": sg_w_out,
            "ffn_w_in": ffn_w_in, "ffn_conv_w": ffn_conv_w, "ffn_w_out": ffn_w_out,
            "ln_g": ln_g, "ln_b": ln_b}


def reference(x, gla_w_in, gla_w_gk2, gla_b_gk, gla_norm_w, gla_w_out,
              gdn_w_in, gdn_conv_w, gdn_a_log, gdn_dt_bias, gdn_norm_w, gdn_w_out,
              sg_w_in, sg_ln_g, sg_ln_b, sg_w_sp, sg_b_sp, sg_w_out,
              ffn_w_in, ffn_conv_w, ffn_w_out, ln_g, ln_b):
    h = x
    for i in range(DEPTH):
        mixer, j = i % N_MIXERS, i // N_MIXERS
        if mixer == 0:
            y = gla_mixer(h, gla_w_in[j], gla_w_gk2[j], gla_b_gk[j], gla_norm_w[j], gla_w_out[j])
        elif mixer == 1:
            y = gdn_mixer(h, gdn_w_in[j], gdn_conv_w[j], gdn_a_log[j], gdn_dt_bias[j],
                          gdn_norm_w[j], gdn_w_out[j])
        else:
            y = sgu_mixer(h, sg_w_in[j], sg_ln_g[j], sg_ln_b[j], sg_w_sp[j], sg_b_sp[j], sg_w_out[j])
        h = layer_norm(ALPHA * h + y, ln_g[i, 0], ln_b[i, 0])
        h = layer_norm(ALPHA * h + conv_ffn(h, ffn_w_in[i], ffn_conv_w[i], ffn_w_out[i]),
                       ln_g[i, 1], ln_b[i, 1])
    return h
```

```python
import functools

import jax
import jax.numpy as jnp
from jax import lax
from jax.experimental import pallas as pl
from jax.experimental.pallas import tpu as pltpu

F32 = jnp.float32
BF16 = jnp.bfloat16

LN_EPS = 1e-5
RMS_EPS = 1e-6
GLA_NORMALIZER = 16.0
CHUNK = 64
BLOCK = 256
SG_CHUNK = 128
HALO = 8
VMEM_LIMIT = 56 * 1024 * 1024


def _mm_general(a, b, dims):
    return lax.dot_general(a, b, (dims, ((), ())), preferred_element_type=F32)


def _mm(a, b):
    return _mm_general(a, b, ((1,), (0,)))


def _dot(a, b):
    return _mm(a.astype(BF16), b.astype(BF16))


def _dot_nt(a, b):
    return _mm_general(a.astype(BF16), b.astype(BF16), ((1,), (1,)))


def _dot_tn(a, b):
    return _mm_general(a.astype(BF16), b.astype(BF16), ((0,), (0,)))


def _dot_split(m_bf16, x):
    hi = x.astype(BF16)
    r1 = x - hi.astype(F32)
    mid = r1.astype(BF16)
    lo = (r1 - mid.astype(F32)).astype(BF16)
    return _mm(m_bf16, hi) + _mm(m_bf16, mid) + _mm(m_bf16, lo)


def _layer_norm(r, g, b):
    mu = jnp.mean(r, -1, keepdims=True)
    c = r - mu
    var = jnp.mean(c * c, -1, keepdims=True)
    return c * lax.rsqrt(var + LN_EPS) * g + b


def _gelu(x):
    return 0.5 * x * (1.0 + lax.erf(x * (2.0 ** -0.5)))


def _silu(x):
    return x * (1.0 / (1.0 + jnp.exp(-x)))


def _sigmoid(x):
    return 1.0 / (1.0 + jnp.exp(-x))


def _softplus(x):
    return jnp.maximum(x, 0.0) + jnp.log(1.0 + jnp.exp(-jnp.abs(x)))


def _block_masks():
    row = lax.broadcasted_iota(jnp.int32, (BLOCK, BLOCK), 0)
    col = lax.broadcasted_iota(jnp.int32, (BLOCK, BLOCK), 1)
    same = (row // CHUNK) == (col // CHUNK)
    incl = same & (col <= row)
    strict = same & (col < row)
    return incl, strict


def _per_chunk_row(x, r):
    parts = []
    for c in range(BLOCK // CHUNK):
        parts.append(jnp.broadcast_to(x[c * CHUNK + r:c * CHUNK + r + 1, :], (CHUNK, x.shape[1])))
    return jnp.concatenate(parts, axis=0)


def _ffn_kernel(x_ref, wg_ref, wu_ref, cg_ref, cu_ref, wo_ref, g_ref, b_ref, o_ref,
                xb_ref, hg_ref, hu_ref, carry_g, carry_u, acc_ref, *, ts, nf, alpha):
    i = pl.program_id(0)
    j = pl.program_id(1)

    @pl.when(j == 0)
    def _():
        xb_ref[...] = x_ref[...].astype(BF16)
        acc_ref[...] = jnp.zeros_like(acc_ref)

    @pl.when(i == 0)
    def _():
        hg_ref[0:HALO, :] = jnp.zeros((HALO, hg_ref.shape[1]), F32)
        hu_ref[0:HALO, :] = jnp.zeros((HALO, hu_ref.shape[1]), F32)

    @pl.when(i > 0)
    def _():
        hg_ref[0:HALO, :] = carry_g[j]
        hu_ref[0:HALO, :] = carry_u[j]

    xb = xb_ref[...]
    hg_ref[HALO:HALO + ts, :] = _mm(xb, wg_ref[...])
    hu_ref[HALO:HALO + ts, :] = _mm(xb, wu_ref[...])
    carry_g[j] = hg_ref[ts:ts + HALO, :]
    carry_u[j] = hu_ref[ts:ts + HALO, :]

    cg = cg_ref[...]
    cu = cu_ref[...]
    k = cg.shape[0]
    gate = cg[0:1, :] * hg_ref[HALO - k + 1:HALO - k + 1 + ts, :]
    up = cu[0:1, :] * hu_ref[HALO - k + 1:HALO - k + 1 + ts, :]
    for t in range(1, k):
        off = HALO - k + 1 + t
        gate = gate + cg[t:t + 1, :] * hg_ref[off:off + ts, :]
        up = up + cu[t:t + 1, :] * hu_ref[off:off + ts, :]
    act = (_gelu(gate) * up).astype(BF16)
    acc_ref[...] += _mm(act, wo_ref[...])

    @pl.when(j == nf - 1)
    def _():
        r = alpha * x_ref[...] + acc_ref[...]
        o_ref[...] = _layer_norm(r, g_ref[...], b_ref[...])


def _ffn_layer(h, w_in, conv_w, w_out, ln_g, ln_b, *, alpha, ts=1024, tf=256):
    s, d = h.shape
    dff = w_out.shape[0]
    nf = dff // tf
    assert s % ts == 0 and dff % tf == 0
    w_in_b = w_in.astype(BF16)
    w_out_b = w_out.astype(BF16)
    kern = functools.partial(_ffn_kernel, ts=ts, nf=nf, alpha=alpha)
    return pl.pallas_call(
        kern,
        grid=(s // ts, nf),
        in_specs=[
            pl.BlockSpec((ts, d), lambda i, j: (i, 0)),
            pl.BlockSpec((d, tf), lambda i, j: (0, j)),
            pl.BlockSpec((d, tf), lambda i, j: (0, j + nf)),
            pl.BlockSpec((conv_w.shape[0], tf), lambda i, j: (0, j)),
            pl.BlockSpec((conv_w.shape[0], tf), lambda i, j: (0, j + nf)),
            pl.BlockSpec((tf, d), lambda i, j: (j, 0)),
            pl.BlockSpec((1, d), lambda i, j: (0, 0)),
            pl.BlockSpec((1, d), lambda i, j: (0, 0)),
        ],
        out_specs=pl.BlockSpec((ts, d), lambda i, j: (i, 0)),
        out_shape=jax.ShapeDtypeStruct((s, d), F32),
        scratch_shapes=[
            pltpu.VMEM((ts, d), BF16),
            pltpu.VMEM((ts + HALO, tf), F32),
            pltpu.VMEM((ts + HALO, tf), F32),
            pltpu.VMEM((nf, HALO, tf), F32),
            pltpu.VMEM((nf, HALO, tf), F32),
            pltpu.VMEM((ts, d), F32),
        ],
        compiler_params=pltpu.CompilerParams(
            dimension_semantics=("arbitrary", "arbitrary"), vmem_limit_bytes=VMEM_LIMIT),
        name="conv_ffn",
    )(h, w_in_b, w_in_b, conv_w, conv_w, w_out_b, ln_g.reshape(1, d), ln_b.reshape(1, d))


def _sgu_kernel(x_ref, wu_ref, wv_ref, lg_ref, lb_ref, wsp_ref, bsp_ref, wo_ref, g_ref, b_ref,
                o_ref, m_ref, *, ts, groups, alpha):
    x = x_ref[...]
    xb = x.astype(BF16)
    u = _gelu(_mm(xb, wu_ref[...]))
    v = _gelu(_mm(xb, wv_ref[...]))
    v = _layer_norm(v, lg_ref[...], lb_ref[...])
    gw = v.shape[1] // groups
    row = lax.broadcasted_iota(jnp.int32, (SG_CHUNK, SG_CHUNK), 0)
    col = lax.broadcasted_iota(jnp.int32, (SG_CHUNK, SG_CHUNK), 1)
    causal = col <= row
    bsp = bsp_ref[...]
    for g in range(groups):
        wg = jnp.where(causal, wsp_ref[g], 0.0).astype(BF16)
        bias = jnp.broadcast_to(bsp[:, g:g + 1], (SG_CHUNK, gw))
        for c in range(ts // SG_CHUNK):
            vv = v[c * SG_CHUNK:(c + 1) * SG_CHUNK, g * gw:(g + 1) * gw].astype(BF16)
            mixed = _mm(wg, vv) + bias
            m_ref[c * SG_CHUNK:(c + 1) * SG_CHUNK, g * gw:(g + 1) * gw] = (
                u[c * SG_CHUNK:(c + 1) * SG_CHUNK, g * gw:(g + 1) * gw] * mixed).astype(BF16)
    y = _mm(m_ref[...], wo_ref[...])
    o_ref[...] = _layer_norm(alpha * x + y, g_ref[...], b_ref[...])


def _sgu_layer(h, w_in, sg_ln_g, sg_ln_b, w_sp, b_sp, w_out, ln_g, ln_b, *, alpha, ts=512):
    s, d = h.shape
    width = w_out.shape[0]
    groups = w_sp.shape[0]
    assert s % ts == 0 and ts % SG_CHUNK == 0
    w_in_b = w_in.astype(BF16)
    full = lambda *shape: pl.BlockSpec(shape, lambda i: (0,) * len(shape))
    kern = functools.partial(_sgu_kernel, ts=ts, groups=groups, alpha=alpha)
    return pl.pallas_call(
        kern,
        grid=(s // ts,),
        in_specs=[
            pl.BlockSpec((ts, d), lambda i: (i, 0)),
            pl.BlockSpec((d, width), lambda i: (0, 0)),
            pl.BlockSpec((d, width), lambda i: (0, 1)),
            full(1, width), full(1, width),
            full(groups, SG_CHUNK, SG_CHUNK),
            full(SG_CHUNK, groups),
            full(width, d), full(1, d), full(1, d),
        ],
        out_specs=pl.BlockSpec((ts, d), lambda i: (i, 0)),
        out_shape=jax.ShapeDtypeStruct((s, d), F32),
        scratch_shapes=[pltpu.VMEM((ts, width), BF16)],
        compiler_params=pltpu.CompilerParams(
            dimension_semantics=("arbitrary",), vmem_limit_bytes=VMEM_LIMIT),
        name="sgu_mixer",
    )(h, w_in_b, w_in_b, sg_ln_g.reshape(1, width), sg_ln_b.reshape(1, width), w_sp, b_sp.T,
      w_out.astype(BF16), ln_g.reshape(1, d), ln_b.reshape(1, d))


def _gla_kernel(x_ref, wq_ref, wk_ref, wv_ref, wg_ref, wlr_ref, wgk2_ref, bgk_ref, nw_ref, wo_ref,
                g_ref, b_ref, o_ref, s_ref, q_s, k_s, v_s, gk_s, og_s, *, ts, heads, dk, dv, alpha):
    i = pl.program_id(0)

    @pl.when(i == 0)
    def _():
        s_ref[...] = jnp.zeros_like(s_ref)

    x = x_ref[...]
    xb = x.astype(BF16)
    q_s[...] = _mm(xb, wq_ref[...]) * (dk ** -0.5)
    k_s[...] = _mm(xb, wk_ref[...])
    v_s[...] = _mm(xb, wv_ref[...])
    lr = _mm(xb, wlr_ref[...])
    z = _dot(lr, wgk2_ref[...]) + bgk_ref[...]
    gk_s[...] = -_softplus(-z) * (1.0 / GLA_NORMALIZER)

    incl, _ = _block_masks()
    tril = jnp.where(incl, 1.0, 0.0).astype(BF16)
    nchunk = BLOCK // CHUNK
    lane = lax.broadcasted_iota(jnp.int32, (dk, BLOCK), 1)

    for blk in range(ts // BLOCK):
        r0 = blk * BLOCK
        bcum_all = _dot_split(tril, gk_s[r0:r0 + BLOCK, :])
        for h in range(heads):
            bc = bcum_all[:, h * dk:(h + 1) * dk]
            bref = _per_chunk_row(bc, CHUNK // 2)
            blast = _per_chunk_row(bc, CHUNK - 1)
            qh = q_s[r0:r0 + BLOCK, h * dk:(h + 1) * dk]
            kh = k_s[r0:r0 + BLOCK, h * dk:(h + 1) * dk]
            vh = v_s[r0:r0 + BLOCK, h * dv:(h + 1) * dv].astype(BF16)
            scores = _dot_nt(qh * jnp.exp(bc - bref), kh * jnp.exp(bref - bc))
            o_intra = _dot(jnp.where(incl, scores, 0.0), vh)
            qd = (qh * jnp.exp(bc)).astype(BF16)
            kdt = (kh * jnp.exp(blast - bc)).T
            bct = bc.T
            state = s_ref[h]
            outs = []
            for c in range(nchunk):
                rows = slice(c * CHUNK, (c + 1) * CHUNK)
                outs.append(o_intra[rows, :] + _dot(qd[rows, :], state))
                in_chunk = (lane >= c * CHUNK) & (lane < (c + 1) * CHUNK)
                d_state = _dot(jnp.where(in_chunk, kdt, 0.0), vh)
                last = c * CHUNK + CHUNK - 1
                state = state * jnp.exp(bct[:, last:last + 1]) + d_state
            s_ref[h] = state
            o = jnp.concatenate(outs, axis=0)
            o = o * lax.rsqrt(jnp.mean(o * o, -1, keepdims=True) + RMS_EPS) * nw_ref[...]
            gate = _mm(xb[r0:r0 + BLOCK, :], wg_ref[:, h * dv:(h + 1) * dv])
            og_s[r0:r0 + BLOCK, h * dv:(h + 1) * dv] = (o * _silu(gate)).astype(BF16)

    y = _mm(og_s[...], wo_ref[...])
    o_ref[...] = _layer_norm(alpha * x + y, g_ref[...], b_ref[...])


def _gla_layer(h, w_in, w_gk2, b_gk, norm_w, w_out, ln_g, ln_b, *, alpha, ts=512):
    s, d = h.shape
    dv = norm_w.shape[0]
    hv = w_out.shape[0]
    heads = hv // dv
    hk = w_gk2.shape[1]
    dk = hk // heads
    rank = w_gk2.shape[0]
    lr_pad = 128
    assert s % ts == 0 and ts % BLOCK == 0
    w_in_b = w_in.astype(BF16)
    wq, wk = w_in_b[:, :hk], w_in_b[:, hk:2 * hk]
    wv, wg = w_in_b[:, 2 * hk:2 * hk + hv], w_in_b[:, 2 * hk + hv:2 * hk + 2 * hv]
    wlr = jnp.pad(w_in_b[:, 2 * hk + 2 * hv:], ((0, 0), (0, lr_pad - rank)))
    wgk2 = jnp.pad(w_gk2.astype(BF16), ((0, lr_pad - rank), (0, 0)))
    full = lambda *shape: pl.BlockSpec(shape, lambda i: (0,) * len(shape))
    kern = functools.partial(_gla_kernel, ts=ts, heads=heads, dk=dk, dv=dv, alpha=alpha)
    return pl.pallas_call(
        kern,
        grid=(s // ts,),
        in_specs=[
            pl.BlockSpec((ts, d), lambda i: (i, 0)),
            full(d, hk), full(d, hk), full(d, hv), full(d, hv), full(d, lr_pad),
            full(lr_pad, hk), full(1, hk), full(1, dv), full(hv, d), full(1, d), full(1, d),
        ],
        out_specs=pl.BlockSpec((ts, d), lambda i: (i, 0)),
        out_shape=jax.ShapeDtypeStruct((s, d), F32),
        scratch_shapes=[
            pltpu.VMEM((heads, dk, dv), F32),
            pltpu.VMEM((ts, hk), F32),
            pltpu.VMEM((ts, hk), F32),
            pltpu.VMEM((ts, hv), F32),
            pltpu.VMEM((ts, hk), F32),
            pltpu.VMEM((ts, hv), BF16),
        ],
        compiler_params=pltpu.CompilerParams(
            dimension_semantics=("arbitrary",), vmem_limit_bytes=VMEM_LIMIT),
        name="gla_mixer",
    )(h, wq, wk, wv, wg, wlr, wgk2, b_gk.reshape(1, hk), norm_w.reshape(1, dv),
      w_out.astype(BF16), ln_g.reshape(1, d), ln_b.reshape(1, d))


def _gdn_kernel(x_ref, wqkv_ref, wg_ref, wb_ref, wa_ref, cw_ref, alog_ref, dtb_ref, nw_ref, wo_ref,
                g_ref, b_ref, o_ref, s_ref, carry_ref, pre_s, qkv_s, u_s, w_s, qk_s, qd_s, kd_s,
                vn_s, o_s, og_s, *, ts, heads, dk, dv, alpha):
    i = pl.program_id(0)
    hk = heads * dk

    @pl.when(i == 0)
    def _():
        s_ref[...] = jnp.zeros_like(s_ref)
        pre_s[0:HALO, :] = jnp.zeros((HALO, pre_s.shape[1]), F32)

    @pl.when(i > 0)
    def _():
        pre_s[0:HALO, :] = carry_ref[...]

    x = x_ref[...]
    xb = x.astype(BF16)
    pre_s[HALO:HALO + ts, :] = _mm(xb, wqkv_ref[...])
    carry_ref[...] = pre_s[ts:ts + HALO, :]
    cw = cw_ref[...]
    kw = cw.shape[0]
    conv = cw[0:1, :] * pre_s[HALO - kw + 1:HALO - kw + 1 + ts, :]
    for t in range(1, kw):
        off = HALO - kw + 1 + t
        conv = conv + cw[t:t + 1, :] * pre_s[off:off + ts, :]
    qkv_s[...] = _silu(conv)

    beta_all = _sigmoid(_mm(xb, wb_ref[...]))
    a_lin = _mm(xb, wa_ref[...])
    g_all = -jnp.exp(alog_ref[...]) * _softplus(a_lin + dtb_ref[...])

    incl, strict = _block_masks()
    tril = jnp.where(incl, 1.0, 0.0).astype(BF16)
    row = lax.broadcasted_iota(jnp.int32, (BLOCK, BLOCK), 0)
    col = lax.broadcasted_iota(jnp.int32, (BLOCK, BLOCK), 1)
    eye = jnp.where(row == col, 1.0, 0.0)
    nchunk = BLOCK // CHUNK

    for blk in range(ts // BLOCK):
        r0 = blk * BLOCK
        bcum = _dot_split(tril, g_all[r0:r0 + BLOCK, :])
        bcum_t = bcum.T
        blast = _per_chunk_row(bcum, CHUNK - 1)
        e_b = jnp.exp(bcum)
        e_kd = jnp.exp(blast - bcum)
        beta_blk = beta_all[r0:r0 + BLOCK, :]
        for h in range(heads):
            q = qkv_s[r0:r0 + BLOCK, h * dk:(h + 1) * dk]
            k = qkv_s[r0:r0 + BLOCK, hk + h * dk:hk + (h + 1) * dk]
            v = qkv_s[r0:r0 + BLOCK, 2 * hk + h * dv:2 * hk + (h + 1) * dv]
            q = q * lax.rsqrt(jnp.sum(q * q, -1, keepdims=True) + RMS_EPS) * (dk ** -0.5)
            k = k * lax.rsqrt(jnp.sum(k * k, -1, keepdims=True) + RMS_EPS)
            beta = beta_blk[:, h:h + 1]
            diff = bcum[:, h:h + 1] - bcum_t[h:h + 1, :]
            decay = jnp.exp(jnp.where(incl, diff, -jnp.inf))
            kb = k * beta
            kbb = kb.astype(BF16)
            kbf = k.astype(BF16)
            a = jnp.where(strict, -(_dot_nt(kbb, kbf) * decay), 0.0)
            t_inv = eye + a
            p = a
            steps = (CHUNK - 1).bit_length() - 1
            for _ in range(steps):
                p = _dot(p, p)
                t_inv = t_inv + _dot(t_inv, p)
            rhs = jnp.concatenate([v * beta, kb * e_b[:, h:h + 1]], axis=1)
            sol = _dot(t_inv, rhs)
            u_s[h] = sol[:, :dv]
            w_s[h] = sol[:, dv:].astype(BF16)
            qk_s[h] = (_dot_nt(q, kbf) * decay).astype(BF16)
            qd_s[h] = (q * e_b[:, h:h + 1]).astype(BF16)
            kd_s[h] = (k * e_kd[:, h:h + 1]).astype(BF16)

        for c in range(nchunk):
            rows = slice(c * CHUNK, (c + 1) * CHUNK)
            last = c * CHUNK + CHUNK - 1
            for h in range(heads):
                state = s_ref[h]
                sb = state.astype(BF16)
                v_new = u_s[h, rows, :] - _mm(w_s[h, rows, :], sb)
                vb = v_new.astype(BF16)
                vn_s[h, rows, :] = vb
                o_s[h, rows, :] = _mm(qd_s[h, rows, :], sb)
                cd = jnp.exp(blast[last:last + 1, h:h + 1])
                s_ref[h] = state * cd + _dot_tn(kd_s[h, rows, :], vb)

        for h in range(heads):
            o = o_s[h] + _mm(qk_s[h], vn_s[h])
            o = o * lax.rsqrt(jnp.mean(o * o, -1, keepdims=True) + RMS_EPS) * nw_ref[...]
            gate = _mm(xb[r0:r0 + BLOCK, :], wg_ref[:, h * dv:(h + 1) * dv])
            og_s[r0:r0 + BLOCK, h * dv:(h + 1) * dv] = (o * _silu(gate)).astype(BF16)

    y = _mm(og_s[...], wo_ref[...])
    o_ref[...] = _layer_norm(alpha * x + y, g_ref[...], b_ref[...])


def _gdn_layer(h, w_in, conv_w, a_log, dt_bias, norm_w, w_out, ln_g, ln_b, *, alpha, ts=512):
    s, d = h.shape
    dv = norm_w.shape[0]
    hv = w_out.shape[0]
    heads = hv // dv
    conv_ch = conv_w.shape[1]
    hk = (conv_ch - hv) // 2
    dk = hk // heads
    pad = 128
    assert s % ts == 0 and ts % BLOCK == 0
    w_in_b = w_in.astype(BF16)
    wqkv = w_in_b[:, :conv_ch]
    wg = w_in_b[:, conv_ch:conv_ch + hv]
    wb = jnp.pad(w_in_b[:, conv_ch + hv:conv_ch + hv + heads], ((0, 0), (0, pad - heads)))
    wa = jnp.pad(w_in_b[:, conv_ch + hv + heads:], ((0, 0), (0, pad - heads)))
    alog = jnp.pad(a_log.reshape(1, heads), ((0, 0), (0, pad - heads)))
    dtb = jnp.pad(dt_bias.reshape(1, heads), ((0, 0), (0, pad - heads)))
    full = lambda *shape: pl.BlockSpec(shape, lambda i: (0,) * len(shape))
    kern = functools.partial(_gdn_kernel, ts=ts, heads=heads, dk=dk, dv=dv, alpha=alpha)
    return pl.pallas_call(
        kern,
        grid=(s // ts,),
        in_specs=[
            pl.BlockSpec((ts, d), lambda i: (i, 0)),
            full(d, conv_ch), full(d, hv), full(d, pad), full(d, pad),
            full(conv_w.shape[0], conv_ch), full(1, pad), full(1, pad), full(1, dv),
            full(hv, d), full(1, d), full(1, d),
        ],
        out_specs=pl.BlockSpec((ts, d), lambda i: (i, 0)),
        out_shape=jax.ShapeDtypeStruct((s, d), F32),
        scratch_shapes=[
            pltpu.VMEM((heads, dk, dv), F32),
            pltpu.VMEM((HALO, conv_ch), F32),
            pltpu.VMEM((ts + HALO, conv_ch), F32),
            pltpu.VMEM((ts, conv_ch), F32),
            pltpu.VMEM((heads, BLOCK, dv), F32),
            pltpu.VMEM((heads, BLOCK, dk), BF16),
            pltpu.VMEM((heads, BLOCK, BLOCK), BF16),
            pltpu.VMEM((heads, BLOCK, dk), BF16),
            pltpu.VMEM((heads, BLOCK, dk), BF16),
            pltpu.VMEM((heads, BLOCK, dv), BF16),
            pltpu.VMEM((heads, BLOCK, dv), F32),
            pltpu.VMEM((ts, hv), BF16),
        ],
        compiler_params=pltpu.CompilerParams(
            dimension_semantics=("arbitrary",), vmem_limit_bytes=VMEM_LIMIT),
        name="gdn_mixer",
    )(h, wqkv, wg, wb, wa, conv_w, alog, dtb, norm_w.reshape(1, dv), w_out.astype(BF16),
      ln_g.reshape(1, d), ln_b.reshape(1, d))


def kernel(x, gla_w_in, gla_w_gk2, gla_b_gk, gla_norm_w, gla_w_out, gdn_w_in, gdn_conv_w, gdn_a_log, gdn_dt_bias, gdn_norm_w, gdn_w_out, sg_w_in, sg_ln_g, sg_ln_b, sg_w_sp, sg_b_sp, sg_w_out, ffn_w_in, ffn_conv_w, ffn_w_out, ln_g, ln_b):
    bsz, s, d = x.shape
    depth = ffn_w_in.shape[0]
    alpha = float((2 * depth) ** 0.25)
    n_mixers = 3
    outs = []
    for bi in range(bsz):
        h = x[bi]
        for i in range(depth):
            mixer, j = i % n_mixers, i // n_mixers
            if mixer == 0:
                h = _gla_layer(h, gla_w_in[j], gla_w_gk2[j], gla_b_gk[j], gla_norm_w[j], gla_w_out[j],
                               ln_g[i, 0], ln_b[i, 0], alpha=alpha)
            elif mixer == 1:
                h = _gdn_layer(h, gdn_w_in[j], gdn_conv_w[j], gdn_a_log[j], gdn_dt_bias[j],
                               gdn_norm_w[j], gdn_w_out[j], ln_g[i, 0], ln_b[i, 0], alpha=alpha)
            else:
                h = _sgu_layer(h, sg_w_in[j], sg_ln_g[j], sg_ln_b[j], sg_w_sp[j], sg_b_sp[j],
                               sg_w_out[j], ln_g[i, 0], ln_b[i, 0], alpha=alpha)
            h = _ffn_layer(h, ffn_w_in[i], ffn_conv_w[i], ffn_w_out[i], ln_g[i, 1], ln_b[i, 1],
                           alpha=alpha)
        outs.append(h)
    return jnp.stack(outs, axis=0)
```

```python
import functools

import jax
import jax.numpy as jnp
from jax import lax
from jax.experimental import pallas as pl
from jax.experimental.pallas import tpu as pltpu

F32 = jnp.float32
BF16 = jnp.bfloat16

LN_EPS = 1e-5
RMS_EPS = 1e-6
GLA_NORMALIZER = 16.0
CHUNK = 64
BLOCK = 256
SG_CHUNK = 128
HALO = 8
VMEM_LIMIT = 56 * 1024 * 1024


def _mm_general(a, b, dims):
    return lax.dot_general(a, b, (dims, ((), ())), preferred_element_type=F32)


def _mm(a, b):
    return _mm_general(a, b, ((1,), (0,)))


def _dot(a, b):
    return _mm(a.astype(BF16), b.astype(BF16))


def _dot_nt(a, b):
    return _mm_general(a.astype(BF16), b.astype(BF16), ((1,), (1,)))


def _dot_tn(a, b):
    return _mm_general(a.astype(BF16), b.astype(BF16), ((0,), (0,)))


def _dot_split(m_bf16, x):
    hi = x.astype(BF16)
    r1 = x - hi.astype(F32)
    mid = r1.astype(BF16)
    lo = (r1 - mid.astype(F32)).astype(BF16)
    return _mm(m_bf16, hi) + _mm(m_bf16, mid) + _mm(m_bf16, lo)


def _layer_norm(r, g, b):
    mu = jnp.mean(r, -1, keepdims=True)
    c = r - mu
    var = jnp.mean(c * c, -1, keepdims=True)
    return c * lax.rsqrt(var + LN_EPS) * g + b


def _gelu(x):
    return 0.5 * x * (1.0 + lax.erf(x * (2.0 ** -0.5)))


def _silu(x):
    return x * (1.0 / (1.0 + jnp.exp(-x)))


def _sigmoid(x):
    return 1.0 / (1.0 + jnp.exp(-x))


def _softplus(x):
    return jnp.maximum(x, 0.0) + jnp.log(1.0 + jnp.exp(-jnp.abs(x)))


def _block_masks():
    row = lax.broadcasted_iota(jnp.int32, (BLOCK, BLOCK), 0)
    col = lax.broadcasted_iota(jnp.int32, (BLOCK, BLOCK), 1)
    same = (row // CHUNK) == (col // CHUNK)
    incl = same & (col <= row)
    strict = same & (col < row)
    return incl, strict


def _per_chunk_row(x, r):
    parts = []
    for c in range(BLOCK // CHUNK):
        parts.append(jnp.broadcast_to(x[c * CHUNK + r:c * CHUNK + r + 1, :], (CHUNK, x.shape[1])))
    return jnp.concatenate(parts, axis=0)


def _ffn_kernel(x_ref, wg_ref, wu_ref, cg_ref, cu_ref, wo_ref, g_ref, b_ref, o_ref,
                xb_ref, hg_ref, hu_ref, carry_g, carry_u, acc_ref, *, ts, nf, alpha):
    i = pl.program_id(0)
    j = pl.program_id(1)

    @pl.when(j == 0)
    def _():
        xb_ref[...] = x_ref[...].astype(BF16)
        acc_ref[...] = jnp.zeros_like(acc_ref)

    @pl.when(i == 0)
    def _():
        hg_ref[0:HALO, :] = jnp.zeros((HALO, hg_ref.shape[1]), F32)
        hu_ref[0:HALO, :] = jnp.zeros((HALO, hu_ref.shape[1]), F32)

    @pl.when(i > 0)
    def _():
        hg_ref[0:HALO, :] = carry_g[j]
        hu_ref[0:HALO, :] = carry_u[j]

    xb = xb_ref[...]
    hg_ref[HALO:HALO + ts, :] = _mm(xb, wg_ref[...])
    hu_ref[HALO:HALO + ts, :] = _mm(xb, wu_ref[...])
    carry_g[j] = hg_ref[ts:ts + HALO, :]
    carry_u[j] = hu_ref[ts:ts + HALO, :]

    cg = cg_ref[...]
    cu = cu_ref[...]
    k = cg.shape[0]
    gate = cg[0:1, :] * hg_ref[HALO - k + 1:HALO - k + 1 + ts, :]
    up = cu[0:1, :] * hu_ref[HALO - k + 1:HALO - k + 1 + ts, :]
    for t in range(1, k):
        off = HALO - k + 1 + t
        gate = gate + cg[t:t + 1, :] * hg_ref[off:off + ts, :]
        up = up + cu[t:t + 1, :] * hu_ref[off:off + ts, :]
    act = (_gelu(gate) * up).astype(BF16)
    acc_ref[...] += _mm(act, wo_ref[...])

    @pl.when(j == nf - 1)
    def _():
        r = alpha * x_ref[...] + acc_ref[...]
        o_ref[...] = _layer_norm(r, g_ref[...], b_ref[...])


def _ffn_layer(h, w_in, conv_w, w_out, ln_g, ln_b, *, alpha, ts=1024, tf=256):
    s, d = h.shape
    dff = w_out.shape[0]
    nf = dff // tf
    assert s % ts == 0 and dff % tf == 0
    w_in_b = w_in.astype(BF16)
    w_out_b = w_out.astype(BF16)
    kern = functools.partial(_ffn_kernel, ts=ts, nf=nf, alpha=alpha)
    return pl.pallas_call(
        kern,
        grid=(s // ts, nf),
        in_specs=[
            pl.BlockSpec((ts, d), lambda i, j: (i, 0)),
            pl.BlockSpec((d, tf), lambda i, j: (0, j)),
            pl.BlockSpec((d, tf), lambda i, j: (0, j + nf)),
            pl.BlockSpec((conv_w.shape[0], tf), lambda i, j: (0, j)),
            pl.BlockSpec((conv_w.shape[0], tf), lambda i, j: (0, j + nf)),
            pl.BlockSpec((tf, d), lambda i, j: (j, 0)),
            pl.BlockSpec((1, d), lambda i, j: (0, 0)),
            pl.BlockSpec((1, d), lambda i, j: (0, 0)),
        ],
        out_specs=pl.BlockSpec((ts, d), lambda i, j: (i, 0)),
        out_shape=jax.ShapeDtypeStruct((s, d), F32),
        scratch_shapes=[
            pltpu.VMEM((ts, d), BF16),
            pltpu.VMEM((ts + HALO, tf), F32),
            pltpu.VMEM((ts + HALO, tf), F32),
            pltpu.VMEM((nf, HALO, tf), F32),
            pltpu.VMEM((nf, HALO, tf), F32),
            pltpu.VMEM((ts, d), F32),
        ],
        compiler_params=pltpu.CompilerParams(
            dimension_semantics=("arbitrary", "arbitrary"), vmem_limit_bytes=VMEM_LIMIT),
        name="conv_ffn",
    )(h, w_in_b, w_in_b, conv_w, conv_w, w_out_b, ln_g.reshape(1, d), ln_b.reshape(1, d))


def _sgu_kernel(x_ref, wu_ref, wv_ref, lg_ref, lb_ref, wsp_ref, bsp_ref, wo_ref, g_ref, b_ref,
                o_ref, m_ref, *, ts, groups, alpha):
    x = x_ref[...]
    xb = x.astype(BF16)
    u = _gelu(_mm(xb, wu_ref[...]))
    v = _gelu(_mm(xb, wv_ref[...]))
    v = _layer_norm(v, lg_ref[...], lb_ref[...])
    gw = v.shape[1] // groups
    row = lax.broadcasted_iota(jnp.int32, (SG_CHUNK, SG_CHUNK), 0)
    col = lax.broadcasted_iota(jnp.int32, (SG_CHUNK, SG_CHUNK), 1)
    causal = col <= row
    bsp = bsp_ref[...]
    for g in range(groups):
        wg = jnp.where(causal, wsp_ref[g], 0.0).astype(BF16)
        bias = jnp.broadcast_to(bsp[:, g:g + 1], (SG_CHUNK, gw))
        for c in range(ts // SG_CHUNK):
            vv = v[c * SG_CHUNK:(c + 1) * SG_CHUNK, g * gw:(g + 1) * gw].astype(BF16)
            mixed = _mm(wg, vv) + bias
            m_ref[c * SG_CHUNK:(c + 1) * SG_CHUNK, g * gw:(g + 1) * gw] = (
                u[c * SG_CHUNK:(c + 1) * SG_CHUNK, g * gw:(g + 1) * gw] * mixed).astype(BF16)
    y = _mm(m_ref[...], wo_ref[...])
    o_ref[...] = _layer_norm(alpha * x + y, g_ref[...], b_ref[...])


def _sgu_layer(h, w_in, sg_ln_g, sg_ln_b, w_sp, b_sp, w_out, ln_g, ln_b, *, alpha, ts=512):
    s, d = h.shape
    width = w_out.shape[0]
    groups = w_sp.shape[0]
    assert s % ts == 0 and ts % SG_CHUNK == 0
    w_in_b = w_in.astype(BF16)
    full = lambda *shape: pl.BlockSpec(shape, lambda i: (0,) * len(shape))
    kern = functools.partial(_sgu_kernel, ts=ts, groups=groups, alpha=alpha)
    return pl.pallas_call(
        kern,
        grid=(s // ts,),
        in_specs=[
            pl.BlockSpec((ts, d), lambda i: (i, 0)),
            pl.BlockSpec((d, width), lambda i: (0, 0)),
            pl.BlockSpec((d, width), lambda i: (0, 1)),
            full(1, width), full(1, width),
            full(groups, SG_CHUNK, SG_CHUNK),
            full(SG_CHUNK, groups),
            full(width, d), full(1, d), full(1, d),
        ],
        out_specs=pl.BlockSpec((ts, d), lambda i: (i, 0)),
        out_shape=jax.ShapeDtypeStruct((s, d), F32),
        scratch_shapes=[pltpu.VMEM((ts, width), BF16)],
        compiler_params=pltpu.CompilerParams(
            dimension_semantics=("arbitrary",), vmem_limit_bytes=VMEM_LIMIT),
        name="sgu_mixer",
    )(h, w_in_b, w_in_b, sg_ln_g.reshape(1, width), sg_ln_b.reshape(1, width), w_sp, b_sp.T,
      w_out.astype(BF16), ln_g.reshape(1, d), ln_b.reshape(1, d))


def _gla_kernel(x_ref, wq_ref, wk_ref, wv_ref, wg_ref, wlr_ref, wgk2_ref, bgk_ref, nw_ref, wo_ref,
                g_ref, b_ref, o_ref, s_ref, q_s, k_s, v_s, gk_s, og_s, *, ts, heads, dk, dv, alpha):
    i = pl.program_id(0)

    @pl.when(i == 0)
    def _():
        s_ref[...] = jnp.zeros_like(s_ref)

    x = x_ref[...]
    xb = x.astype(BF16)
    q_s[...] = _mm(xb, wq_ref[...]) * (dk ** -0.5)
    k_s[...] = _mm(xb, wk_ref[...])
    v_s[...] = _mm(xb, wv_ref[...])
    lr = _mm(xb, wlr_ref[...])
    z = _dot(lr, wgk2_ref[...]) + bgk_ref[...]
    gk_s[...] = -_softplus(-z) * (1.0 / GLA_NORMALIZER)

    incl, _ = _block_masks()
    tril = jnp.where(incl, 1.0, 0.0).astype(BF16)
    nchunk = BLOCK // CHUNK
    lane = lax.broadcasted_iota(jnp.int32, (dk, BLOCK), 1)

    for blk in range(ts // BLOCK):
        r0 = blk * BLOCK
        bcum_all = _dot_split(tril, gk_s[r0:r0 + BLOCK, :])
        for h in range(heads):
            bc = bcum_all[:, h * dk:(h + 1) * dk]
            bref = _per_chunk_row(bc, CHUNK // 2)
            blast = _per_chunk_row(bc, CHUNK - 1)
            qh = q_s[r0:r0 + BLOCK, h * dk:(h + 1) * dk]
            kh = k_s[r0:r0 + BLOCK, h * dk:(h + 1) * dk]
            vh = v_s[r0:r0 + BLOCK, h * dv:(h + 1) * dv].astype(BF16)
            scores = _dot_nt(qh * jnp.exp(bc - bref), kh * jnp.exp(bref - bc))
            o_intra = _dot(jnp.where(incl, scores, 0.0), vh)
            qd = (qh * jnp.exp(bc)).astype(BF16)
            kdt = (kh * jnp.exp(blast - bc)).T
            bct = bc.T
            state = s_ref[h]
            outs = []
            for c in range(nchunk):
                rows = slice(c * CHUNK, (c + 1) * CHUNK)
                outs.append(o_intra[rows, :] + _dot(qd[rows, :], state))
                in_chunk = (lane >= c * CHUNK) & (lane < (c + 1) * CHUNK)
                d_state = _dot(jnp.where(in_chunk, kdt, 0.0), vh)
                last = c * CHUNK + CHUNK - 1
                state = state * jnp.exp(bct[:, last:last + 1]) + d_state
            s_ref[h] = state
            o = jnp.concatenate(outs, axis=0)
            o = o * lax.rsqrt(jnp.mean(o * o, -1, keepdims=True) + RMS_EPS) * nw_ref[...]
            gate = _mm(xb[r0:r0 + BLOCK, :], wg_ref[:, h * dv:(h + 1) * dv])
            og_s[r0:r0 + BLOCK, h * dv:(h + 1) * dv] = (o * _silu(gate)).astype(BF16)

    y = _mm(og_s[...], wo_ref[...])
    o_ref[...] = _layer_norm(alpha * x + y, g_ref[...], b_ref[...])


def _gla_layer(h, w_in, w_gk2, b_gk, norm_w, w_out, ln_g, ln_b, *, alpha, ts=512):
    s, d = h.shape
    dv = norm_w.shape[0]
    hv = w_out.shape[0]
    heads = hv // dv
    hk = w_gk2.shape[1]
    dk = hk // heads
    rank = w_gk2.shape[0]
    lr_pad = 128
    assert s % ts == 0 and ts % BLOCK == 0
    w_in_b = w_in.astype(BF16)
    wq, wk = w_in_b[:, :hk], w_in_b[:, hk:2 * hk]
    wv, wg = w_in_b[:, 2 * hk:2 * hk + hv], w_in_b[:, 2 * hk + hv:2 * hk + 2 * hv]
    wlr = jnp.pad(w_in_b[:, 2 * hk + 2 * hv:], ((0, 0), (0, lr_pad - rank)))
    wgk2 = jnp.pad(w_gk2.astype(BF16), ((0, lr_pad - rank), (0, 0)))
    full = lambda *shape: pl.BlockSpec(shape, lambda i: (0,) * len(shape))
    kern = functools.partial(_gla_kernel, ts=ts, heads=heads, dk=dk, dv=dv, alpha=alpha)
    return pl.pallas_call(
        kern,
        grid=(s // ts,),
        in_specs=[
            pl.BlockSpec((ts, d), lambda i: (i, 0)),
            full(d, hk), full(d, hk), full(d, hv), full(d, hv), full(d, lr_pad),
            full(lr_pad, hk), full(1, hk), full(1, dv), full(hv, d), full(1, d), full(1, d),
        ],
        out_specs=pl.BlockSpec((ts, d), lambda i: (i, 0)),
        out_shape=jax.ShapeDtypeStruct((s, d), F32),
        scratch_shapes=[
            pltpu.VMEM((heads, dk, dv), F32),
            pltpu.VMEM((ts, hk), F32),
            pltpu.VMEM((ts, hk), F32),
            pltpu.VMEM((ts, hv), F32),
            pltpu.VMEM((ts, hk), F32),
            pltpu.VMEM((ts, hv), BF16),
        ],
        compiler_params=pltpu.CompilerParams(
            dimension_semantics=("arbitrary",), vmem_limit_bytes=VMEM_LIMIT),
        name="gla_mixer",
    )(h, wq, wk, wv, wg, wlr, wgk2, b_gk.reshape(1, hk), norm_w.reshape(1, dv),
      w_out.astype(BF16), ln_g.reshape(1, d), ln_b.reshape(1, d))


def _gdn_kernel(x_ref, wqkv_ref, wg_ref, wb_ref, wa_ref, cw_ref, alog_ref, dtb_ref, nw_ref, wo_ref,
                g_ref, b_ref, o_ref, s_ref, carry_ref, pre_s, qkv_s, gate_s, beta_s, glog_s, x_s, p_s,
                rhs_s, u_s, w_s, qk_s, qd_s, kd_s, vn_s, o_s, og_s, *, ts, heads, dk, dv, alpha):
    i = pl.program_id(0)
    hk = heads * dk

    @pl.when(i == 0)
    def _():
        s_ref[...] = jnp.zeros_like(s_ref)
        pre_s[0:HALO, :] = jnp.zeros((HALO, pre_s.shape[1]), F32)

    @pl.when(i > 0)
    def _():
        pre_s[0:HALO, :] = carry_ref[...]

    x = x_ref[...]
    xb = x.astype(BF16)
    pre_s[HALO:HALO + ts, :] = _mm(xb, wqkv_ref[...])
    carry_ref[...] = pre_s[ts:ts + HALO, :]
    cw = cw_ref[...]
    kw = cw.shape[0]
    conv = cw[0:1, :] * pre_s[HALO - kw + 1:HALO - kw + 1 + ts, :]
    for t in range(1, kw):
        off = HALO - kw + 1 + t
        conv = conv + cw[t:t + 1, :] * pre_s[off:off + ts, :]
    qkv_s[...] = _silu(conv)

    gate_s[...] = _mm(xb, wg_ref[...])
    beta_s[...] = _sigmoid(_mm(xb, wb_ref[...]))
    a_lin = _mm(xb, wa_ref[...])
    glog_s[...] = -jnp.exp(alog_ref[...]) * _softplus(a_lin + dtb_ref[...])

    nchunk = BLOCK // CHUNK
    steps = (CHUNK - 1).bit_length() - 1

    def block_body(blk, carry):
        r0 = pl.multiple_of(blk * BLOCK, BLOCK)
        rows_blk = pl.ds(r0, BLOCK)
        incl, strict = _block_masks()
        tril = jnp.where(incl, 1.0, 0.0).astype(BF16)
        row = lax.broadcasted_iota(jnp.int32, (BLOCK, BLOCK), 0)
        col = lax.broadcasted_iota(jnp.int32, (BLOCK, BLOCK), 1)
        bcum = _dot_split(tril, glog_s[rows_blk, :])
        bcum_t = bcum.T
        blast = _per_chunk_row(bcum, CHUNK - 1)
        e_b = jnp.exp(bcum)
        e_kd = jnp.exp(blast - bcum)
        beta_blk = beta_s[rows_blk, :]

        for h in range(heads):
            q = qkv_s[rows_blk, h * dk:(h + 1) * dk]
            k = qkv_s[rows_blk, hk + h * dk:hk + (h + 1) * dk]
            v = qkv_s[rows_blk, 2 * hk + h * dv:2 * hk + (h + 1) * dv]
            q = q * lax.rsqrt(jnp.sum(q * q, -1, keepdims=True) + RMS_EPS) * (dk ** -0.5)
            k = k * lax.rsqrt(jnp.sum(k * k, -1, keepdims=True) + RMS_EPS)
            beta = beta_blk[:, h:h + 1]
            diff = bcum[:, h:h + 1] - bcum_t[h:h + 1, :]
            decay = jnp.exp(jnp.where(incl, diff, -jnp.inf))
            kb = k * beta
            kbf = k.astype(BF16)
            a = jnp.where(strict, -(_dot_nt(kb, kbf) * decay), 0.0)
            p_s[h] = a.astype(BF16)
            x_s[h] = jnp.where(row == col, 1.0, a)
            rhs_s[h, :, 0:dv] = (v * beta).astype(BF16)
            rhs_s[h, :, dv:dv + dk] = (kb * e_b[:, h:h + 1]).astype(BF16)
            qk_s[h] = (_dot_nt(q, kbf) * decay).astype(BF16)
            qd_s[h] = (q * e_b[:, h:h + 1]).astype(BF16)
            kd_s[h] = (k * e_kd[:, h:h + 1]).astype(BF16)

        for step in range(steps):
            for h in range(heads):
                pb = _mm(p_s[h], p_s[h]).astype(BF16)
                if step + 1 < steps:
                    p_s[h] = pb
                xh = x_s[h]
                x_s[h] = xh + _mm(xh.astype(BF16), pb)

        for h in range(heads):
            sol = _mm(x_s[h].astype(BF16), rhs_s[h])
            u_s[h] = sol[:, :dv]
            w_s[h] = sol[:, dv:].astype(BF16)

        for c in range(nchunk):
            rows = slice(c * CHUNK, (c + 1) * CHUNK)
            last = c * CHUNK + CHUNK - 1
            for h in range(heads):
                state = s_ref[h]
                sb = state.astype(BF16)
                v_new = u_s[h, rows, :] - _mm(w_s[h, rows, :], sb)
                vb = v_new.astype(BF16)
                vn_s[h, rows, :] = vb
                o_s[h, rows, :] = _mm(qd_s[h, rows, :], sb)
                cd = jnp.exp(blast[last:last + 1, h:h + 1])
                s_ref[h] = state * cd + _dot_tn(kd_s[h, rows, :], vb)

        for h in range(heads):
            o = o_s[h] + _mm(qk_s[h], vn_s[h])
            o = o * lax.rsqrt(jnp.mean(o * o, -1, keepdims=True) + RMS_EPS) * nw_ref[...]
            gate = gate_s[rows_blk, h * dv:(h + 1) * dv]
            og_s[rows_blk, h * dv:(h + 1) * dv] = (o * _silu(gate)).astype(BF16)
        return carry

    lax.fori_loop(0, ts // BLOCK, block_body, 0)

    y = _mm(og_s[...], wo_ref[...])
    o_ref[...] = _layer_norm(alpha * x + y, g_ref[...], b_ref[...])


def _gdn_layer(h, w_in, conv_w, a_log, dt_bias, norm_w, w_out, ln_g, ln_b, *, alpha, ts=512):
    s, d = h.shape
    dv = norm_w.shape[0]
    hv = w_out.shape[0]
    heads = hv // dv
    conv_ch = conv_w.shape[1]
    hk = (conv_ch - hv) // 2
    dk = hk // heads
    pad = 128
    assert s % ts == 0 and ts % BLOCK == 0
    w_in_b = w_in.astype(BF16)
    wqkv = w_in_b[:, :conv_ch]
    wg = w_in_b[:, conv_ch:conv_ch + hv]
    wb = jnp.pad(w_in_b[:, conv_ch + hv:conv_ch + hv + heads], ((0, 0), (0, pad - heads)))
    wa = jnp.pad(w_in_b[:, conv_ch + hv + heads:], ((0, 0), (0, pad - heads)))
    alog = jnp.pad(a_log.reshape(1, heads), ((0, 0), (0, pad - heads)))
    dtb = jnp.pad(dt_bias.reshape(1, heads), ((0, 0), (0, pad - heads)))
    full = lambda *shape: pl.BlockSpec(shape, lambda i: (0,) * len(shape))
    kern = functools.partial(_gdn_kernel, ts=ts, heads=heads, dk=dk, dv=dv, alpha=alpha)
    return pl.pallas_call(
        kern,
        grid=(s // ts,),
        in_specs=[
            pl.BlockSpec((ts, d), lambda i: (i, 0)),
            full(d, conv_ch), full(d, hv), full(d, pad), full(d, pad),
            full(conv_w.shape[0], conv_ch), full(1, pad), full(1, pad), full(1, dv),
            full(hv, d), full(1, d), full(1, d),
        ],
        out_specs=pl.BlockSpec((ts, d), lambda i: (i, 0)),
        out_shape=jax.ShapeDtypeStruct((s, d), F32),
        scratch_shapes=[
            pltpu.VMEM((heads, dk, dv), F32),
            pltpu.VMEM((HALO, conv_ch), F32),
            pltpu.VMEM((ts + HALO, conv_ch), F32),
            pltpu.VMEM((ts, conv_ch), F32),
            pltpu.VMEM((ts, hv), F32),
            pltpu.VMEM((ts, pad), F32),
            pltpu.VMEM((ts, pad), F32),
            pltpu.VMEM((heads, BLOCK, BLOCK), F32),
            pltpu.VMEM((heads, BLOCK, BLOCK), BF16),
            pltpu.VMEM((heads, BLOCK, dv + dk), BF16),
            pltpu.VMEM((heads, BLOCK, dv), F32),
            pltpu.VMEM((heads, BLOCK, dk), BF16),
            pltpu.VMEM((heads, BLOCK, BLOCK), BF16),
            pltpu.VMEM((heads, BLOCK, dk), BF16),
            pltpu.VMEM((heads, BLOCK, dk), BF16),
            pltpu.VMEM((heads, BLOCK, dv), BF16),
            pltpu.VMEM((heads, BLOCK, dv), F32),
            pltpu.VMEM((ts, hv), BF16),
        ],
        compiler_params=pltpu.CompilerParams(
            dimension_semantics=("arbitrary",), vmem_limit_bytes=VMEM_LIMIT),
        name="gdn_mixer",
    )(h, wqkv, wg, wb, wa, conv_w, alog, dtb, norm_w.reshape(1, dv), w_out.astype(BF16),
      ln_g.reshape(1, d), ln_b.reshape(1, d))


def kernel(x, gla_w_in, gla_w_gk2, gla_b_gk, gla_norm_w, gla_w_out, gdn_w_in, gdn_conv_w, gdn_a_log, gdn_dt_bias, gdn_norm_w, gdn_w_out, sg_w_in, sg_ln_g, sg_ln_b, sg_w_sp, sg_b_sp, sg_w_out, ffn_w_in, ffn_conv_w, ffn_w_out, ln_g, ln_b):
    bsz, s, d = x.shape
    depth = ffn_w_in.shape[0]
    alpha = float((2 * depth) ** 0.25)
    n_mixers = 3
    outs = []
    for bi in range(bsz):
        h = x[bi]
        for i in range(depth):
            mixer, j = i % n_mixers, i // n_mixers
            if mixer == 0:
                h = _gla_layer(h, gla_w_in[j], gla_w_gk2[j], gla_b_gk[j], gla_norm_w[j], gla_w_out[j],
                               ln_g[i, 0], ln_b[i, 0], alpha=alpha)
            elif mixer == 1:
                h = _gdn_layer(h, gdn_w_in[j], gdn_conv_w[j], gdn_a_log[j], gdn_dt_bias[j],
                               gdn_norm_w[j], gdn_w_out[j], ln_g[i, 0], ln_b[i, 0], alpha=alpha)
            else:
                h = _sgu_layer(h, sg_w_in[j], sg_ln_g[j], sg_ln_b[j], sg_w_sp[j], sg_b_sp[j],
                               sg_w_out[j], ln_g[i, 0], ln_b[i, 0], alpha=alpha)
            h = _ffn_layer(h, ffn_w_in[i], ffn_conv_w[i], ffn_w_out[i], ln_g[i, 1], ln_b[i, 1],
                           alpha=alpha)
        outs.append(h)
    return jnp.stack(outs, axis=0)
```

```python
import functools

import jax
import jax.numpy as jnp
from jax import lax
from jax.experimental import pallas as pl
from jax.experimental.pallas import tpu as pltpu

F32 = jnp.float32
BF16 = jnp.bfloat16

LN_EPS = 1e-5
RMS_EPS = 1e-6
GLA_NORMALIZER = 16.0
CHUNK = 64
BLOCK = 256
SG_CHUNK = 128
HALO = 8
VMEM_LIMIT = 56 * 1024 * 1024


def _mm_general(a, b, dims):
    return lax.dot_general(a, b, (dims, ((), ())), preferred_element_type=F32)


def _mm(a, b):
    return _mm_general(a, b, ((1,), (0,)))


def _dot(a, b):
    return _mm(a.astype(BF16), b.astype(BF16))


def _dot_nt(a, b):
    return _mm_general(a.astype(BF16), b.astype(BF16), ((1,), (1,)))


def _dot_tn(a, b):
    return _mm_general(a.astype(BF16), b.astype(BF16), ((0,), (0,)))


def _dot_split(m_bf16, x):
    hi = x.astype(BF16)
    r1 = x - hi.astype(F32)
    mid = r1.astype(BF16)
    lo = (r1 - mid.astype(F32)).astype(BF16)
    return _mm(m_bf16, hi) + _mm(m_bf16, mid) + _mm(m_bf16, lo)


def _layer_norm(r, g, b):
    mu = jnp.mean(r, -1, keepdims=True)
    c = r - mu
    var = jnp.mean(c * c, -1, keepdims=True)
    return c * lax.rsqrt(var + LN_EPS) * g + b


def _gelu(x):
    return 0.5 * x * (1.0 + lax.erf(x * (2.0 ** -0.5)))


def _silu(x):
    return x * (1.0 / (1.0 + jnp.exp(-x)))


def _sigmoid(x):
    return 1.0 / (1.0 + jnp.exp(-x))


def _softplus(x):
    return jnp.maximum(x, 0.0) + jnp.log(1.0 + jnp.exp(-jnp.abs(x)))


def _block_masks():
    row = lax.broadcasted_iota(jnp.int32, (BLOCK, BLOCK), 0)
    col = lax.broadcasted_iota(jnp.int32, (BLOCK, BLOCK), 1)
    same = (row // CHUNK) == (col // CHUNK)
    incl = same & (col <= row)
    strict = same & (col < row)
    return incl, strict


def _per_chunk_row(x, r):
    parts = []
    for c in range(BLOCK // CHUNK):
        parts.append(jnp.broadcast_to(x[c * CHUNK + r:c * CHUNK + r + 1, :], (CHUNK, x.shape[1])))
    return jnp.concatenate(parts, axis=0)


def _conv_taps(buf_ref, slot, cw, ts):
    k = cw.shape[0]
    out = cw[0:1, :] * buf_ref[slot, HALO - k + 1:HALO - k + 1 + ts, :]
    for t in range(1, k):
        off = HALO - k + 1 + t
        out = out + cw[t:t + 1, :] * buf_ref[slot, off:off + ts, :]
    return out


def _project_with_history(buf_ref, slot, carry_ref, xb, w_ref, col0, width, ts):
    buf_ref[slot, 0:HALO, :] = carry_ref[:, col0:col0 + width]
    buf_ref[slot, HALO:HALO + ts, :] = _mm(xb, w_ref[:, col0:col0 + width])
    carry_ref[:, col0:col0 + width] = buf_ref[slot, ts:ts + HALO, :]


def _ffn_kernel(x_ref, wi_ref, cw_ref, wo_ref, g_ref, b_ref, o_ref, hg_s, hu_s, carry_s, act_s,
                *, ts, tf, nf, alpha):
    @pl.when(pl.program_id(0) == 0)
    def _():
        carry_s[...] = jnp.zeros_like(carry_s)

    x = x_ref[...]
    xb = x.astype(BF16)
    dff = nf * tf
    for j in range(nf):
        slot = j % 2
        _project_with_history(hg_s, slot, carry_s, xb, wi_ref, j * tf, tf, ts)
        _project_with_history(hu_s, slot, carry_s, xb, wi_ref, dff + j * tf, tf, ts)
        gate = _conv_taps(hg_s, slot, cw_ref[:, j * tf:(j + 1) * tf], ts)
        up = _conv_taps(hu_s, slot, cw_ref[:, dff + j * tf:dff + (j + 1) * tf], ts)
        act_s[:, j * tf:(j + 1) * tf] = (_gelu(gate) * up).astype(BF16)
    y = _mm(act_s[...], wo_ref[...])
    o_ref[...] = _layer_norm(alpha * x + y, g_ref[...], b_ref[...])


def _resident(*shape):
    return pl.BlockSpec(shape, lambda i: (0,) * len(shape), pipeline_mode=pl.Buffered(1))


def _ffn_layer(h, w_in, conv_w, w_out, ln_g, ln_b, *, alpha, ts=512, tf=256):
    s, d = h.shape
    dff = w_out.shape[0]
    nf = dff // tf
    assert s % ts == 0 and dff % tf == 0
    kern = functools.partial(_ffn_kernel, ts=ts, tf=tf, nf=nf, alpha=alpha)
    return pl.pallas_call(
        kern,
        grid=(s // ts,),
        in_specs=[
            pl.BlockSpec((ts, d), lambda i: (i, 0)),
            _resident(d, 2 * dff), _resident(conv_w.shape[0], 2 * dff), _resident(dff, d),
            _resident(1, d), _resident(1, d),
        ],
        out_specs=pl.BlockSpec((ts, d), lambda i: (i, 0)),
        out_shape=jax.ShapeDtypeStruct((s, d), F32),
        scratch_shapes=[
            pltpu.VMEM((2, ts + HALO, tf), F32),
            pltpu.VMEM((2, ts + HALO, tf), F32),
            pltpu.VMEM((HALO, 2 * dff), F32),
            pltpu.VMEM((ts, dff), BF16),
        ],
        compiler_params=pltpu.CompilerParams(
            dimension_semantics=("arbitrary",), vmem_limit_bytes=VMEM_LIMIT),
        name="conv_ffn",
    )(h, w_in.astype(BF16), conv_w, w_out.astype(BF16), ln_g.reshape(1, d), ln_b.reshape(1, d))


def _sgu_kernel(x_ref, wu_ref, wv_ref, lg_ref, lb_ref, wsp_ref, bsp_ref, wo_ref, g_ref, b_ref,
                o_ref, m_ref, *, ts, groups, alpha):
    x = x_ref[...]
    xb = x.astype(BF16)
    u = _gelu(_mm(xb, wu_ref[...]))
    v = _gelu(_mm(xb, wv_ref[...]))
    v = _layer_norm(v, lg_ref[...], lb_ref[...])
    gw = v.shape[1] // groups
    row = lax.broadcasted_iota(jnp.int32, (SG_CHUNK, SG_CHUNK), 0)
    col = lax.broadcasted_iota(jnp.int32, (SG_CHUNK, SG_CHUNK), 1)
    causal = col <= row
    bsp = bsp_ref[...]
    for g in range(groups):
        wg = jnp.where(causal, wsp_ref[g], 0.0).astype(BF16)
        bias = jnp.broadcast_to(bsp[:, g:g + 1], (SG_CHUNK, gw))
        for c in range(ts // SG_CHUNK):
            vv = v[c * SG_CHUNK:(c + 1) * SG_CHUNK, g * gw:(g + 1) * gw].astype(BF16)
            mixed = _mm(wg, vv) + bias
            m_ref[c * SG_CHUNK:(c + 1) * SG_CHUNK, g * gw:(g + 1) * gw] = (
                u[c * SG_CHUNK:(c + 1) * SG_CHUNK, g * gw:(g + 1) * gw] * mixed).astype(BF16)
    y = _mm(m_ref[...], wo_ref[...])
    o_ref[...] = _layer_norm(alpha * x + y, g_ref[...], b_ref[...])


def _sgu_layer(h, w_in, sg_ln_g, sg_ln_b, w_sp, b_sp, w_out, ln_g, ln_b, *, alpha, ts=512):
    s, d = h.shape
    width = w_out.shape[0]
    groups = w_sp.shape[0]
    assert s % ts == 0 and ts % SG_CHUNK == 0
    w_in_b = w_in.astype(BF16)
    full = _resident
    kern = functools.partial(_sgu_kernel, ts=ts, groups=groups, alpha=alpha)
    return pl.pallas_call(
        kern,
        grid=(s // ts,),
        in_specs=[
            pl.BlockSpec((ts, d), lambda i: (i, 0)),
            pl.BlockSpec((d, width), lambda i: (0, 0)),
            pl.BlockSpec((d, width), lambda i: (0, 1)),
            full(1, width), full(1, width),
            full(groups, SG_CHUNK, SG_CHUNK),
            full(SG_CHUNK, groups),
            full(width, d), full(1, d), full(1, d),
        ],
        out_specs=pl.BlockSpec((ts, d), lambda i: (i, 0)),
        out_shape=jax.ShapeDtypeStruct((s, d), F32),
        scratch_shapes=[pltpu.VMEM((ts, width), BF16)],
        compiler_params=pltpu.CompilerParams(
            dimension_semantics=("arbitrary",), vmem_limit_bytes=VMEM_LIMIT),
        name="sgu_mixer",
    )(h, w_in_b, w_in_b, sg_ln_g.reshape(1, width), sg_ln_b.reshape(1, width), w_sp, b_sp.T,
      w_out.astype(BF16), ln_g.reshape(1, d), ln_b.reshape(1, d))


def _gla_kernel(x_ref, wq_ref, wk_ref, wv_ref, wg_ref, wlr_ref, wgk2_ref, bgk_ref, nw_ref, wo_ref,
                g_ref, b_ref, o_ref, s_ref, q_s, k_s, v_s, gk_s, gate_s, vb_s, oi_s, qd_s, kd_s, dec_s,
                og_s, *, ts, heads, dk, dv, alpha):
    i = pl.program_id(0)

    @pl.when(i == 0)
    def _():
        s_ref[...] = jnp.zeros_like(s_ref)

    x = x_ref[...]
    xb = x.astype(BF16)
    q_s[...] = _mm(xb, wq_ref[...]) * (dk ** -0.5)
    k_s[...] = _mm(xb, wk_ref[...])
    v_s[...] = _mm(xb, wv_ref[...])
    lr = _mm(xb, wlr_ref[...])
    z = _dot(lr, wgk2_ref[...]) + bgk_ref[...]
    gk_s[...] = -_softplus(-z) * (1.0 / GLA_NORMALIZER)

    gate_s[...] = _mm(xb, wg_ref[...])

    incl, _ = _block_masks()
    tril = jnp.where(incl, 1.0, 0.0).astype(BF16)
    nchunk = BLOCK // CHUNK

    for blk in range(ts // BLOCK):
        r0 = blk * BLOCK
        bcum_all = _dot_split(tril, gk_s[r0:r0 + BLOCK, :])
        for h in range(heads):
            bc = bcum_all[:, h * dk:(h + 1) * dk]
            bref = _per_chunk_row(bc, CHUNK // 2)
            blast = _per_chunk_row(bc, CHUNK - 1)
            qh = q_s[r0:r0 + BLOCK, h * dk:(h + 1) * dk]
            kh = k_s[r0:r0 + BLOCK, h * dk:(h + 1) * dk]
            vh = v_s[r0:r0 + BLOCK, h * dv:(h + 1) * dv].astype(BF16)
            vb_s[h] = vh
            scores = _dot_nt(qh * jnp.exp(bc - bref), kh * jnp.exp(bref - bc))
            oi_s[h] = _dot(jnp.where(incl, scores, 0.0), vh)
            qd_s[h] = (qh * jnp.exp(bc)).astype(BF16)
            kd_s[h] = (kh * jnp.exp(blast - bc)).astype(BF16)
            dec_s[h] = jnp.exp(bc.T)

        for c in range(nchunk):
            rows = slice(c * CHUNK, (c + 1) * CHUNK)
            last = c * CHUNK + CHUNK - 1
            for h in range(heads):
                state = s_ref[h]
                oi_s[h, rows, :] += _mm(qd_s[h, rows, :], state.astype(BF16))
                d_state = _dot_tn(kd_s[h, rows, :], vb_s[h, rows, :])
                s_ref[h] = state * dec_s[h, :, last:last + 1] + d_state

        for h in range(heads):
            o = oi_s[h]
            o = o * lax.rsqrt(jnp.mean(o * o, -1, keepdims=True) + RMS_EPS) * nw_ref[...]
            gate = gate_s[r0:r0 + BLOCK, h * dv:(h + 1) * dv]
            og_s[r0:r0 + BLOCK, h * dv:(h + 1) * dv] = (o * _silu(gate)).astype(BF16)

    y = _mm(og_s[...], wo_ref[...])
    o_ref[...] = _layer_norm(alpha * x + y, g_ref[...], b_ref[...])


def _gla_layer(h, w_in, w_gk2, b_gk, norm_w, w_out, ln_g, ln_b, *, alpha, ts=512):
    s, d = h.shape
    dv = norm_w.shape[0]
    hv = w_out.shape[0]
    heads = hv // dv
    hk = w_gk2.shape[1]
    dk = hk // heads
    rank = w_gk2.shape[0]
    lr_pad = 128
    assert s % ts == 0 and ts % BLOCK == 0
    w_in_b = w_in.astype(BF16)
    wq, wk = w_in_b[:, :hk], w_in_b[:, hk:2 * hk]
    wv, wg = w_in_b[:, 2 * hk:2 * hk + hv], w_in_b[:, 2 * hk + hv:2 * hk + 2 * hv]
    wlr = jnp.pad(w_in_b[:, 2 * hk + 2 * hv:], ((0, 0), (0, lr_pad - rank)))
    wgk2 = jnp.pad(w_gk2.astype(BF16), ((0, lr_pad - rank), (0, 0)))
    full = _resident
    kern = functools.partial(_gla_kernel, ts=ts, heads=heads, dk=dk, dv=dv, alpha=alpha)
    return pl.pallas_call(
        kern,
        grid=(s // ts,),
        in_specs=[
            pl.BlockSpec((ts, d), lambda i: (i, 0)),
            full(d, hk), full(d, hk), full(d, hv), full(d, hv), full(d, lr_pad),
            full(lr_pad, hk), full(1, hk), full(1, dv), full(hv, d), full(1, d), full(1, d),
        ],
        out_specs=pl.BlockSpec((ts, d), lambda i: (i, 0)),
        out_shape=jax.ShapeDtypeStruct((s, d), F32),
        scratch_shapes=[
            pltpu.VMEM((heads, dk, dv), F32),
            pltpu.VMEM((ts, hk), F32),
            pltpu.VMEM((ts, hk), F32),
            pltpu.VMEM((ts, hv), F32),
            pltpu.VMEM((ts, hk), F32),
            pltpu.VMEM((ts, hv), F32),
            pltpu.VMEM((heads, BLOCK, dv), BF16),
            pltpu.VMEM((heads, BLOCK, dv), F32),
            pltpu.VMEM((heads, BLOCK, dk), BF16),
            pltpu.VMEM((heads, BLOCK, dk), BF16),
            pltpu.VMEM((heads, dk, BLOCK), F32),
            pltpu.VMEM((ts, hv), BF16),
        ],
        compiler_params=pltpu.CompilerParams(
            dimension_semantics=("arbitrary",), vmem_limit_bytes=VMEM_LIMIT),
        name="gla_mixer",
    )(h, wq, wk, wv, wg, wlr, wgk2, b_gk.reshape(1, hk), norm_w.reshape(1, dv),
      w_out.astype(BF16), ln_g.reshape(1, d), ln_b.reshape(1, d))


def _gdn_kernel(x_ref, wqkv_ref, wg_ref, wb_ref, wa_ref, cw_ref, alog_ref, dtb_ref, nw_ref, wo_ref,
                g_ref, b_ref, o_ref, s_ref, carry_ref, pre_s, qkv_s, gate_s, beta_s, glog_s, x_s, p_s,
                rhs_s, u_s, w_s, qk_s, qd_s, kd_s, vn_s, o_s, og_s, *, ts, heads, dk, dv, alpha):
    i = pl.program_id(0)
    hk = heads * dk

    @pl.when(i == 0)
    def _():
        s_ref[...] = jnp.zeros_like(s_ref)
        carry_ref[...] = jnp.zeros_like(carry_ref)

    x = x_ref[...]
    xb = x.astype(BF16)
    tc = pre_s.shape[2]
    for j in range(qkv_s.shape[1] // tc):
        slot = j % 2
        _project_with_history(pre_s, slot, carry_ref, xb, wqkv_ref, j * tc, tc, ts)
        qkv_s[:, j * tc:(j + 1) * tc] = _silu(_conv_taps(pre_s, slot, cw_ref[:, j * tc:(j + 1) * tc], ts))

    gate_s[...] = _mm(xb, wg_ref[...])
    beta_s[...] = _sigmoid(_mm(xb, wb_ref[...]))
    a_lin = _mm(xb, wa_ref[...])
    glog_s[...] = -jnp.exp(alog_ref[...]) * _softplus(a_lin + dtb_ref[...])

    nchunk = BLOCK // CHUNK
    steps = (CHUNK - 1).bit_length() - 1

    def block_body(blk, carry):
        rows_blk = pl.ds(pl.multiple_of(blk * BLOCK, BLOCK), BLOCK)
        incl, strict = _block_masks()
        tril = jnp.where(incl, 1.0, 0.0).astype(BF16)
        row = lax.broadcasted_iota(jnp.int32, (BLOCK, BLOCK), 0)
        col = lax.broadcasted_iota(jnp.int32, (BLOCK, BLOCK), 1)
        bcum = _dot_split(tril, glog_s[rows_blk, :])
        bcum_t = bcum.T
        blast = _per_chunk_row(bcum, CHUNK - 1)
        e_b = jnp.exp(bcum)
        e_kd = jnp.exp(blast - bcum)
        beta_blk = beta_s[rows_blk, :]

        for h in range(heads):
            q = qkv_s[rows_blk, h * dk:(h + 1) * dk]
            k = qkv_s[rows_blk, hk + h * dk:hk + (h + 1) * dk]
            v = qkv_s[rows_blk, 2 * hk + h * dv:2 * hk + (h + 1) * dv]
            q = q * lax.rsqrt(jnp.sum(q * q, -1, keepdims=True) + RMS_EPS) * (dk ** -0.5)
            k = k * lax.rsqrt(jnp.sum(k * k, -1, keepdims=True) + RMS_EPS)
            beta = beta_blk[:, h:h + 1]
            diff = bcum[:, h:h + 1] - bcum_t[h:h + 1, :]
            decay = jnp.exp(jnp.where(incl, diff, -jnp.inf))
            kb = k * beta
            kbf = k.astype(BF16)
            a = jnp.where(strict, -(_dot_nt(kb, kbf) * decay), 0.0)
            p_s[h] = a.astype(BF16)
            x_s[h] = jnp.where(row == col, 1.0, a)
            rhs_s[h, :, 0:dv] = (v * beta).astype(BF16)
            rhs_s[h, :, dv:dv + dk] = (kb * e_b[:, h:h + 1]).astype(BF16)
            qk_s[h] = (_dot_nt(q, kbf) * decay).astype(BF16)
            qd_s[h] = (q * e_b[:, h:h + 1]).astype(BF16)
            kd_s[h] = (k * e_kd[:, h:h + 1]).astype(BF16)

        for step in range(steps):
            for h in range(heads):
                pb = _mm(p_s[h], p_s[h]).astype(BF16)
                if step + 1 < steps:
                    p_s[h] = pb
                xh = x_s[h]
                x_s[h] = xh + _mm(xh.astype(BF16), pb)

        for h in range(heads):
            sol = _mm(x_s[h].astype(BF16), rhs_s[h])
            u_s[h] = sol[:, :dv]
            w_s[h] = sol[:, dv:].astype(BF16)

        for c in range(nchunk):
            rows = slice(c * CHUNK, (c + 1) * CHUNK)
            last = c * CHUNK + CHUNK - 1
            for h in range(heads):
                state = s_ref[h]
                sb = state.astype(BF16)
                v_new = u_s[h, rows, :] - _mm(w_s[h, rows, :], sb)
                vb = v_new.astype(BF16)
                vn_s[h, rows, :] = vb
                o_s[h, rows, :] = _mm(qd_s[h, rows, :], sb)
                cd = jnp.exp(blast[last:last + 1, h:h + 1])
                s_ref[h] = state * cd + _dot_tn(kd_s[h, rows, :], vb)

        for h in range(heads):
            o = o_s[h] + _mm(qk_s[h], vn_s[h])
            o = o * lax.rsqrt(jnp.mean(o * o, -1, keepdims=True) + RMS_EPS) * nw_ref[...]
            gate = gate_s[rows_blk, h * dv:(h + 1) * dv]
            og_s[rows_blk, h * dv:(h + 1) * dv] = (o * _silu(gate)).astype(BF16)
        return carry

    lax.fori_loop(0, ts // BLOCK, block_body, 0)

    y = _mm(og_s[...], wo_ref[...])
    o_ref[...] = _layer_norm(alpha * x + y, g_ref[...], b_ref[...])


def _gdn_layer(h, w_in, conv_w, a_log, dt_bias, norm_w, w_out, ln_g, ln_b, *, alpha, ts=512):
    s, d = h.shape
    dv = norm_w.shape[0]
    hv = w_out.shape[0]
    heads = hv // dv
    conv_ch = conv_w.shape[1]
    hk = (conv_ch - hv) // 2
    dk = hk // heads
    pad = 128
    assert s % ts == 0 and ts % BLOCK == 0
    w_in_b = w_in.astype(BF16)
    wqkv = w_in_b[:, :conv_ch]
    wg = w_in_b[:, conv_ch:conv_ch + hv]
    wb = jnp.pad(w_in_b[:, conv_ch + hv:conv_ch + hv + heads], ((0, 0), (0, pad - heads)))
    wa = jnp.pad(w_in_b[:, conv_ch + hv + heads:], ((0, 0), (0, pad - heads)))
    alog = jnp.pad(a_log.reshape(1, heads), ((0, 0), (0, pad - heads)))
    dtb = jnp.pad(dt_bias.reshape(1, heads), ((0, 0), (0, pad - heads)))
    full = _resident
    kern = functools.partial(_gdn_kernel, ts=ts, heads=heads, dk=dk, dv=dv, alpha=alpha)
    return pl.pallas_call(
        kern,
        grid=(s // ts,),
        in_specs=[
            pl.BlockSpec((ts, d), lambda i: (i, 0)),
            full(d, conv_ch), full(d, hv), full(d, pad), full(d, pad),
            full(conv_w.shape[0], conv_ch), full(1, pad), full(1, pad), full(1, dv),
            full(hv, d), full(1, d), full(1, d),
        ],
        out_specs=pl.BlockSpec((ts, d), lambda i: (i, 0)),
        out_shape=jax.ShapeDtypeStruct((s, d), F32),
        scratch_shapes=[
            pltpu.VMEM((heads, dk, dv), F32),
            pltpu.VMEM((HALO, conv_ch), F32),
            pltpu.VMEM((2, ts + HALO, 512), F32),
            pltpu.VMEM((ts, conv_ch), F32),
            pltpu.VMEM((ts, hv), F32),
            pltpu.VMEM((ts, pad), F32),
            pltpu.VMEM((ts, pad), F32),
            pltpu.VMEM((heads, BLOCK, BLOCK), F32),
            pltpu.VMEM((heads, BLOCK, BLOCK), BF16),
            pltpu.VMEM((heads, BLOCK, dv + dk), BF16),
            pltpu.VMEM((heads, BLOCK, dv), F32),
            pltpu.VMEM((heads, BLOCK, dk), BF16),
            pltpu.VMEM((heads, BLOCK, BLOCK), BF16),
            pltpu.VMEM((heads, BLOCK, dk), BF16),
            pltpu.VMEM((heads, BLOCK, dk), BF16),
            pltpu.VMEM((heads, BLOCK, dv), BF16),
            pltpu.VMEM((heads, BLOCK, dv), F32),
            pltpu.VMEM((ts, hv), BF16),
        ],
        compiler_params=pltpu.CompilerParams(
            dimension_semantics=("arbitrary",), vmem_limit_bytes=VMEM_LIMIT),
        name="gdn_mixer",
    )(h, wqkv, wg, wb, wa, conv_w, alog, dtb, norm_w.reshape(1, dv), w_out.astype(BF16),
      ln_g.reshape(1, d), ln_b.reshape(1, d))


def kernel(x, gla_w_in, gla_w_gk2, gla_b_gk, gla_norm_w, gla_w_out, gdn_w_in, gdn_conv_w, gdn_a_log, gdn_dt_bias, gdn_norm_w, gdn_w_out, sg_w_in, sg_ln_g, sg_ln_b, sg_w_sp, sg_b_sp, sg_w_out, ffn_w_in, ffn_conv_w, ffn_w_out, ln_g, ln_b):
    bsz, s, d = x.shape
    depth = ffn_w_in.shape[0]
    alpha = float((2 * depth) ** 0.25)
    n_mixers = 3
    outs = []
    for bi in range(bsz):
        h = x[bi]
        for i in range(depth):
            mixer, j = i % n_mixers, i // n_mixers
            if mixer == 0:
                h = _gla_layer(h, gla_w_in[j], gla_w_gk2[j], gla_b_gk[j], gla_norm_w[j], gla_w_out[j],
                               ln_g[i, 0], ln_b[i, 0], alpha=alpha)
            elif mixer == 1:
                h = _gdn_layer(h, gdn_w_in[j], gdn_conv_w[j], gdn_a_log[j], gdn_dt_bias[j],
                               gdn_norm_w[j], gdn_w_out[j], ln_g[i, 0], ln_b[i, 0], alpha=alpha)
            else:
                h = _sgu_layer(h, sg_w_in[j], sg_ln_g[j], sg_ln_b[j], sg_w_sp[j], sg_b_sp[j],
                               sg_w_out[j], ln_g[i, 0], ln_b[i, 0], alpha=alpha)
            h = _ffn_layer(h, ffn_w_in[i], ffn_conv_w[i], ffn_w_out[i], ln_g[i, 1], ln_b[i, 1],
                           alpha=alpha)
        outs.append(h)
    return jnp.stack(outs, axis=0)
```

```python
import functools

import jax
import jax.numpy as jnp
from jax import lax
from jax.experimental import pallas as pl
from jax.experimental.pallas import tpu as pltpu

F32 = jnp.float32
BF16 = jnp.bfloat16

LN_EPS = 1e-5
RMS_EPS = 1e-6
GLA_NORMALIZER = 16.0
CHUNK = 64
BLOCK = 256
SG_CHUNK = 128
SUBLANES = 8
LANES = 128
HALO = SUBLANES
VMEM_LIMIT = 56 * 1024 * 1024


def _mm_general(a, b, dims):
    return lax.dot_general(a, b, (dims, ((), ())), preferred_element_type=F32)


def _mm(a, b):
    return _mm_general(a, b, ((1,), (0,)))


def _dot(a, b):
    return _mm(a.astype(BF16), b.astype(BF16))


def _dot_nt(a, b):
    return _mm_general(a.astype(BF16), b.astype(BF16), ((1,), (1,)))


def _dot_tn(a, b):
    return _mm_general(a.astype(BF16), b.astype(BF16), ((0,), (0,)))


def _dot_split(m_bf16, x):
    hi = x.astype(BF16)
    r1 = x - hi.astype(F32)
    mid = r1.astype(BF16)
    lo = (r1 - mid.astype(F32)).astype(BF16)
    return _mm(m_bf16, hi) + _mm(m_bf16, mid) + _mm(m_bf16, lo)


def _layer_norm(r, g, b):
    mu = jnp.mean(r, -1, keepdims=True)
    c = r - mu
    var = jnp.mean(c * c, -1, keepdims=True)
    return c * lax.rsqrt(var + LN_EPS) * g + b


def _gelu(x):
    return 0.5 * x * (1.0 + lax.erf(x * (2.0 ** -0.5)))


def _silu(x):
    return x * (1.0 / (1.0 + jnp.exp(-x)))


def _sigmoid(x):
    return 1.0 / (1.0 + jnp.exp(-x))


def _softplus(x):
    return jnp.maximum(x, 0.0) + jnp.log(1.0 + jnp.exp(-jnp.abs(x)))


def _block_masks():
    row = lax.broadcasted_iota(jnp.int32, (BLOCK, BLOCK), 0)
    col = lax.broadcasted_iota(jnp.int32, (BLOCK, BLOCK), 1)
    same = (row // CHUNK) == (col // CHUNK)
    incl = same & (col <= row)
    strict = same & (col < row)
    return incl, strict


def _per_chunk_row(x, r):
    parts = []
    for c in range(BLOCK // CHUNK):
        parts.append(jnp.broadcast_to(x[c * CHUNK + r:c * CHUNK + r + 1, :], (CHUNK, x.shape[1])))
    return jnp.concatenate(parts, axis=0)


def _conv_taps(buf_ref, slot, cw, ts):
    k = cw.shape[0]
    out = cw[0:1, :] * buf_ref[slot, HALO - k + 1:HALO - k + 1 + ts, :]
    for t in range(1, k):
        off = HALO - k + 1 + t
        out = out + cw[t:t + 1, :] * buf_ref[slot, off:off + ts, :]
    return out


def _project_with_history(buf_ref, slot, carry_ref, xb, w_ref, col0, width, ts):
    buf_ref[slot, 0:HALO, :] = carry_ref[:, col0:col0 + width]
    buf_ref[slot, HALO:HALO + ts, :] = _mm(xb, w_ref[:, col0:col0 + width])
    carry_ref[:, col0:col0 + width] = buf_ref[slot, ts:ts + HALO, :]


def _restride(stage_ref, slot, val, starts, stride):
    tiles = val.shape[1] // LANES
    for c in range(tiles):
        stage_ref[slot, c] = val[:, c * LANES:(c + 1) * LANES]
    rows = []
    for st in starts:
        rows.append(jnp.concatenate(
            [stage_ref[slot, c, pl.ds(st, SUBLANES, stride=stride), :] for c in range(tiles)], axis=1))
    return jnp.concatenate(rows, axis=0)


def _to_interleaved(stage_ref, slot, val):
    n = val.shape[0] // SUBLANES
    return _restride(stage_ref, slot, val, range(n), n)


def _from_interleaved(stage_ref, slot, val):
    n = val.shape[0] // SUBLANES
    starts = [SUBLANES * ((SUBLANES * j) % n) + (SUBLANES * j) // n for j in range(n)]
    return _restride(stage_ref, slot, val, starts, SUBLANES)


def _project_interleaved(buf_ref, slot, carry_ref, xb, w_ref, col0, width, ts, taps):
    hist = SUBLANES * (taps - 1)
    buf_ref[slot, hist:hist + ts, :] = _mm(xb, w_ref[:, col0:col0 + width])
    first = lax.broadcasted_iota(jnp.int32, (SUBLANES, width), 0) == 0
    for m in range(1, taps):
        at = hist - SUBLANES * m
        cur = buf_ref[slot, at + ts:at + ts + SUBLANES, :]
        prev = carry_ref[at:at + SUBLANES, col0:col0 + width]
        buf_ref[slot, at:at + SUBLANES, :] = jnp.where(
            first, pltpu.roll(prev, 1, axis=0), pltpu.roll(cur, 1, axis=0))
    carry_ref[:, col0:col0 + width] = buf_ref[slot, ts:ts + hist, :]


def _conv_interleaved(buf_ref, slot, cw, ts):
    out = cw[0:1, :] * buf_ref[slot, 0:ts, :]
    for t in range(1, cw.shape[0]):
        out = out + cw[t:t + 1, :] * buf_ref[slot, SUBLANES * t:SUBLANES * t + ts, :]
    return out


def _ffn_kernel(x_ref, wi_ref, cw_ref, wo_ref, g_ref, b_ref, o_ref, hg_s, hu_s, carry_s, act_s,
                stage_s, *, ts, sub, tf, nf, alpha):
    @pl.when(pl.program_id(0) == 0)
    def _():
        carry_s[...] = jnp.zeros_like(carry_s)

    dff = nf * tf
    taps = cw_ref.shape[0]
    for u in range(ts // sub):
        rows = slice(u * sub, (u + 1) * sub)
        x = _to_interleaved(stage_s, u, x_ref[rows, :])
        xb = x.astype(BF16)
        for j in range(nf):
            slot = 2 * u + j % 2
            _project_interleaved(hg_s, slot, carry_s, xb, wi_ref, j * tf, tf, sub, taps)
            _project_interleaved(hu_s, slot, carry_s, xb, wi_ref, dff + j * tf, tf, sub, taps)
            gate = _conv_interleaved(hg_s, slot, cw_ref[:, j * tf:(j + 1) * tf], sub)
            up = _conv_interleaved(hu_s, slot, 0.5 * cw_ref[:, dff + j * tf:dff + (j + 1) * tf], sub)
            act = gate * (1.0 + lax.erf(gate * (2.0 ** -0.5))) * up
            act_s[u, :, j * tf:(j + 1) * tf] = act.astype(BF16)
        y = _mm(act_s[u], wo_ref[...])
        o_ref[rows, :] = _from_interleaved(
            stage_s, u, _layer_norm(alpha * x + y, g_ref[...], b_ref[...]))


def _resident(*shape):
    return pl.BlockSpec(shape, lambda i: (0,) * len(shape), pipeline_mode=pl.Buffered(1))


def _ffn_layer(h, w_in, conv_w, w_out, ln_g, ln_b, *, alpha, ts=512, sub=512, tf=256):
    s, d = h.shape
    dff = w_out.shape[0]
    nf = dff // tf
    nsub = ts // sub
    assert s % ts == 0 and ts % sub == 0 and dff % tf == 0 and sub % (SUBLANES * SUBLANES) == 0
    hist = SUBLANES * (conv_w.shape[0] - 1)
    kern = functools.partial(_ffn_kernel, ts=ts, sub=sub, tf=tf, nf=nf, alpha=alpha)
    return pl.pallas_call(
        kern,
        grid=(s // ts,),
        in_specs=[
            pl.BlockSpec((ts, d), lambda i: (i, 0)),
            _resident(d, 2 * dff), _resident(conv_w.shape[0], 2 * dff), _resident(dff, d),
            _resident(1, d), _resident(1, d),
        ],
        out_specs=pl.BlockSpec((ts, d), lambda i: (i, 0)),
        out_shape=jax.ShapeDtypeStruct((s, d), F32),
        scratch_shapes=[
            pltpu.VMEM((2 * nsub, sub + hist, tf), F32),
            pltpu.VMEM((2 * nsub, sub + hist, tf), F32),
            pltpu.VMEM((hist, 2 * dff), F32),
            pltpu.VMEM((nsub, sub, dff), BF16),
            pltpu.VMEM((nsub, d // LANES, sub, LANES), F32),
        ],
        compiler_params=pltpu.CompilerParams(
            dimension_semantics=("arbitrary",), vmem_limit_bytes=VMEM_LIMIT),
        name="conv_ffn",
    )(h, w_in.astype(BF16), conv_w, w_out.astype(BF16), ln_g.reshape(1, d), ln_b.reshape(1, d))


def _sgu_kernel(x_ref, wu_ref, wv_ref, lg_ref, lb_ref, wsp_ref, bsp_ref, wo_ref, g_ref, b_ref,
                o_ref, m_ref, *, ts, groups, alpha):
    x = x_ref[...]
    xb = x.astype(BF16)
    u = _gelu(_mm(xb, wu_ref[...]))
    v = _gelu(_mm(xb, wv_ref[...]))
    v = _layer_norm(v, lg_ref[...], lb_ref[...])
    gw = v.shape[1] // groups
    row = lax.broadcasted_iota(jnp.int32, (SG_CHUNK, SG_CHUNK), 0)
    col = lax.broadcasted_iota(jnp.int32, (SG_CHUNK, SG_CHUNK), 1)
    causal = col <= row
    bsp = bsp_ref[...]
    for g in range(groups):
        wg = jnp.where(causal, wsp_ref[g], 0.0).astype(BF16)
        bias = jnp.broadcast_to(bsp[:, g:g + 1], (SG_CHUNK, gw))
        for c in range(ts // SG_CHUNK):
            vv = v[c * SG_CHUNK:(c + 1) * SG_CHUNK, g * gw:(g + 1) * gw].astype(BF16)
            mixed = _mm(wg, vv) + bias
            m_ref[c * SG_CHUNK:(c + 1) * SG_CHUNK, g * gw:(g + 1) * gw] = (
                u[c * SG_CHUNK:(c + 1) * SG_CHUNK, g * gw:(g + 1) * gw] * mixed).astype(BF16)
    y = _mm(m_ref[...], wo_ref[...])
    o_ref[...] = _layer_norm(alpha * x + y, g_ref[...], b_ref[...])


def _sgu_layer(h, w_in, sg_ln_g, sg_ln_b, w_sp, b_sp, w_out, ln_g, ln_b, *, alpha, ts=512):
    s, d = h.shape
    width = w_out.shape[0]
    groups = w_sp.shape[0]
    assert s % ts == 0 and ts % SG_CHUNK == 0
    w_in_b = w_in.astype(BF16)
    full = _resident
    kern = functools.partial(_sgu_kernel, ts=ts, groups=groups, alpha=alpha)
    return pl.pallas_call(
        kern,
        grid=(s // ts,),
        in_specs=[
            pl.BlockSpec((ts, d), lambda i: (i, 0)),
            pl.BlockSpec((d, width), lambda i: (0, 0)),
            pl.BlockSpec((d, width), lambda i: (0, 1)),
            full(1, width), full(1, width),
            full(groups, SG_CHUNK, SG_CHUNK),
            full(SG_CHUNK, groups),
            full(width, d), full(1, d), full(1, d),
        ],
        out_specs=pl.BlockSpec((ts, d), lambda i: (i, 0)),
        out_shape=jax.ShapeDtypeStruct((s, d), F32),
        scratch_shapes=[pltpu.VMEM((ts, width), BF16)],
        compiler_params=pltpu.CompilerParams(
            dimension_semantics=("arbitrary",), vmem_limit_bytes=VMEM_LIMIT),
        name="sgu_mixer",
    )(h, w_in_b, w_in_b, sg_ln_g.reshape(1, width), sg_ln_b.reshape(1, width), w_sp, b_sp.T,
      w_out.astype(BF16), ln_g.reshape(1, d), ln_b.reshape(1, d))


def _gla_kernel(x_ref, w_ref, wlr_ref, wgk2_ref, bgk_ref, nw_ref, wo_ref,
                g_ref, b_ref, o_ref, s_ref, q_s, k_s, v_s, gk_s, gate_s, vb_s, oi_s, qd_s, kd_s, dec_s,
                og_s, *, ts, heads, dk, dv, alpha):
    i = pl.program_id(0)

    @pl.when(i == 0)
    def _():
        s_ref[...] = jnp.zeros_like(s_ref)

    x = x_ref[...]
    xb = x.astype(BF16)
    hk, hv = heads * dk, heads * dv
    q_s[...] = _mm(xb, w_ref[:, 0:hk]) * (dk ** -0.5)
    k_s[...] = _mm(xb, w_ref[:, hk:2 * hk])
    v_s[...] = _mm(xb, w_ref[:, 2 * hk:2 * hk + hv])
    lr = _mm(xb, wlr_ref[...])
    z = _dot(lr, wgk2_ref[...]) + bgk_ref[...]
    gk_s[...] = -_softplus(-z) * (1.0 / GLA_NORMALIZER)

    gate_s[...] = _mm(xb, w_ref[:, 2 * hk + hv:2 * hk + 2 * hv])

    incl, _ = _block_masks()
    tril = jnp.where(incl, 1.0, 0.0).astype(BF16)
    nchunk = BLOCK // CHUNK

    for blk in range(ts // BLOCK):
        r0 = blk * BLOCK
        bcum_all = _dot_split(tril, gk_s[r0:r0 + BLOCK, :])
        for h in range(heads):
            bc = bcum_all[:, h * dk:(h + 1) * dk]
            bref = _per_chunk_row(bc, CHUNK // 2)
            blast = _per_chunk_row(bc, CHUNK - 1)
            qh = q_s[r0:r0 + BLOCK, h * dk:(h + 1) * dk]
            kh = k_s[r0:r0 + BLOCK, h * dk:(h + 1) * dk]
            vh = v_s[r0:r0 + BLOCK, h * dv:(h + 1) * dv].astype(BF16)
            vb_s[h] = vh
            scores = _dot_nt(qh * jnp.exp(bc - bref), kh * jnp.exp(bref - bc))
            oi_s[h] = _dot(jnp.where(incl, scores, 0.0), vh)
            qd_s[h] = (qh * jnp.exp(bc)).astype(BF16)
            kd_s[h] = (kh * jnp.exp(blast - bc)).astype(BF16)
            dec_s[h] = jnp.exp(bc.T)

        for c in range(nchunk):
            rows = slice(c * CHUNK, (c + 1) * CHUNK)
            last = c * CHUNK + CHUNK - 1
            for h in range(heads):
                state = s_ref[h]
                oi_s[h, rows, :] += _mm(qd_s[h, rows, :], state.astype(BF16))
                d_state = _dot_tn(kd_s[h, rows, :], vb_s[h, rows, :])
                s_ref[h] = state * dec_s[h, :, last:last + 1] + d_state

        for h in range(heads):
            o = oi_s[h]
            o = o * lax.rsqrt(jnp.mean(o * o, -1, keepdims=True) + RMS_EPS) * nw_ref[...]
            gate = gate_s[r0:r0 + BLOCK, h * dv:(h + 1) * dv]
            og_s[r0:r0 + BLOCK, h * dv:(h + 1) * dv] = (o * _silu(gate)).astype(BF16)

    y = _mm(og_s[...], wo_ref[...])
    o_ref[...] = _layer_norm(alpha * x + y, g_ref[...], b_ref[...])


def _gla_layer(h, w_in, w_gk2, b_gk, norm_w, w_out, ln_g, ln_b, *, alpha, ts=512):
    s, d = h.shape
    dv = norm_w.shape[0]
    hv = w_out.shape[0]
    heads = hv // dv
    hk = w_gk2.shape[1]
    dk = hk // heads
    rank = w_gk2.shape[0]
    lr_pad = 128
    assert s % ts == 0 and ts % BLOCK == 0
    w_in_b = w_in.astype(BF16)
    wlr = jnp.pad(w_in_b[:, 2 * hk + 2 * hv:], ((0, 0), (0, lr_pad - rank)))
    wgk2 = jnp.pad(w_gk2.astype(BF16), ((0, lr_pad - rank), (0, 0)))
    full = _resident
    kern = functools.partial(_gla_kernel, ts=ts, heads=heads, dk=dk, dv=dv, alpha=alpha)
    return pl.pallas_call(
        kern,
        grid=(s // ts,),
        in_specs=[
            pl.BlockSpec((ts, d), lambda i: (i, 0)),
            full(d, w_in.shape[1]), full(d, lr_pad),
            full(lr_pad, hk), full(1, hk), full(1, dv), full(hv, d), full(1, d), full(1, d),
        ],
        out_specs=pl.BlockSpec((ts, d), lambda i: (i, 0)),
        out_shape=jax.ShapeDtypeStruct((s, d), F32),
        scratch_shapes=[
            pltpu.VMEM((heads, dk, dv), F32),
            pltpu.VMEM((ts, hk), F32),
            pltpu.VMEM((ts, hk), F32),
            pltpu.VMEM((ts, hv), F32),
            pltpu.VMEM((ts, hk), F32),
            pltpu.VMEM((ts, hv), F32),
            pltpu.VMEM((heads, BLOCK, dv), BF16),
            pltpu.VMEM((heads, BLOCK, dv), F32),
            pltpu.VMEM((heads, BLOCK, dk), BF16),
            pltpu.VMEM((heads, BLOCK, dk), BF16),
            pltpu.VMEM((heads, dk, BLOCK), F32),
            pltpu.VMEM((ts, hv), BF16),
        ],
        compiler_params=pltpu.CompilerParams(
            dimension_semantics=("arbitrary",), vmem_limit_bytes=VMEM_LIMIT),
        name="gla_mixer",
    )(h, w_in_b, wlr, wgk2, b_gk.reshape(1, hk), norm_w.reshape(1, dv),
      w_out.astype(BF16), ln_g.reshape(1, d), ln_b.reshape(1, d))


def _gdn_kernel(x_ref, w_ref, wb_ref, wa_ref, cw_ref, alog_ref, dtb_ref, nw_ref, wo_ref,
                g_ref, b_ref, o_ref, s_ref, carry_ref, pre_s, qkv_s, gate_s, beta_s, glog_s, x_s, p_s,
                rhs_s, u_s, w_s, qk_s, qd_s, kd_s, vn_s, o_s, og_s, *, ts, heads, dk, dv, alpha):
    i = pl.program_id(0)
    hk = heads * dk

    @pl.when(i == 0)
    def _():
        s_ref[...] = jnp.zeros_like(s_ref)
        carry_ref[...] = jnp.zeros_like(carry_ref)

    x = x_ref[...]
    xb = x.astype(BF16)
    tc = pre_s.shape[2]
    for j in range(qkv_s.shape[1] // tc):
        slot = j % 2
        _project_with_history(pre_s, slot, carry_ref, xb, w_ref, j * tc, tc, ts)
        qkv_s[:, j * tc:(j + 1) * tc] = _silu(_conv_taps(pre_s, slot, cw_ref[:, j * tc:(j + 1) * tc], ts))

    conv_ch = qkv_s.shape[1]
    gate_s[...] = _mm(xb, w_ref[:, conv_ch:conv_ch + heads * dv])
    beta_s[...] = _sigmoid(_mm(xb, wb_ref[...]))
    a_lin = _mm(xb, wa_ref[...])
    glog_s[...] = -jnp.exp(alog_ref[...]) * _softplus(a_lin + dtb_ref[...])

    nchunk = BLOCK // CHUNK
    steps = (CHUNK - 1).bit_length() - 1

    def block_body(blk, carry):
        rows_blk = pl.ds(pl.multiple_of(blk * BLOCK, BLOCK), BLOCK)
        incl, strict = _block_masks()
        tril = jnp.where(incl, 1.0, 0.0).astype(BF16)
        row = lax.broadcasted_iota(jnp.int32, (BLOCK, BLOCK), 0)
        col = lax.broadcasted_iota(jnp.int32, (BLOCK, BLOCK), 1)
        bcum = _dot_split(tril, glog_s[rows_blk, :])
        bcum_t = bcum.T
        blast = _per_chunk_row(bcum, CHUNK - 1)
        e_b = jnp.exp(bcum)
        e_kd = jnp.exp(blast - bcum)
        beta_blk = beta_s[rows_blk, :]

        for h in range(heads):
            q = qkv_s[rows_blk, h * dk:(h + 1) * dk]
            k = qkv_s[rows_blk, hk + h * dk:hk + (h + 1) * dk]
            v = qkv_s[rows_blk, 2 * hk + h * dv:2 * hk + (h + 1) * dv]
            q = q * lax.rsqrt(jnp.sum(q * q, -1, keepdims=True) + RMS_EPS) * (dk ** -0.5)
            k = k * lax.rsqrt(jnp.sum(k * k, -1, keepdims=True) + RMS_EPS)
            beta = beta_blk[:, h:h + 1]
            diff = bcum[:, h:h + 1] - bcum_t[h:h + 1, :]
            decay = jnp.exp(jnp.where(incl, diff, -jnp.inf))
            kb = k * beta
            kbf = k.astype(BF16)
            a = jnp.where(strict, -(_dot_nt(kb, kbf) * decay), 0.0)
            p_s[h] = a.astype(BF16)
            x_s[h] = jnp.where(row == col, 1.0, a)
            rhs_s[h, :, 0:dv] = (v * beta).astype(BF16)
            rhs_s[h, :, dv:dv + dk] = (kb * e_b[:, h:h + 1]).astype(BF16)
            qk_s[h] = (_dot_nt(q, kbf) * decay).astype(BF16)
            qd_s[h] = (q * e_b[:, h:h + 1]).astype(BF16)
            kd_s[h] = (k * e_kd[:, h:h + 1]).astype(BF16)

        for step in range(steps):
            for h in range(heads):
                pb = _mm(p_s[h], p_s[h]).astype(BF16)
                if step + 1 < steps:
                    p_s[h] = pb
                xh = x_s[h]
                x_s[h] = xh + _mm(xh.astype(BF16), pb)

        for h in range(heads):
            sol = _mm(x_s[h].astype(BF16), rhs_s[h])
            u_s[h] = sol[:, :dv]
            w_s[h] = sol[:, dv:].astype(BF16)

        for c in range(nchunk):
            rows = slice(c * CHUNK, (c + 1) * CHUNK)
            last = c * CHUNK + CHUNK - 1
            for h in range(heads):
                state = s_ref[h]
                sb = state.astype(BF16)
                v_new = u_s[h, rows, :] - _mm(w_s[h, rows, :], sb)
                vb = v_new.astype(BF16)
                vn_s[h, rows, :] = vb
                o_s[h, rows, :] = _mm(qd_s[h, rows, :], sb)
                cd = jnp.exp(blast[last:last + 1, h:h + 1])
                s_ref[h] = state * cd + _dot_tn(kd_s[h, rows, :], vb)

        for h in range(heads):
            o = o_s[h] + _mm(qk_s[h], vn_s[h])
            o = o * lax.rsqrt(jnp.mean(o * o, -1, keepdims=True) + RMS_EPS) * nw_ref[...]
            gate = gate_s[rows_blk, h * dv:(h + 1) * dv]
            og_s[rows_blk, h * dv:(h + 1) * dv] = (o * _silu(gate)).astype(BF16)
        return carry

    lax.fori_loop(0, ts // BLOCK, block_body, 0)

    y = _mm(og_s[...], wo_ref[...])
    o_ref[...] = _layer_norm(alpha * x + y, g_ref[...], b_ref[...])


def _gdn_layer(h, w_in, conv_w, a_log, dt_bias, norm_w, w_out, ln_g, ln_b, *, alpha, ts=512):
    s, d = h.shape
    dv = norm_w.shape[0]
    hv = w_out.shape[0]
    heads = hv // dv
    conv_ch = conv_w.shape[1]
    hk = (conv_ch - hv) // 2
    dk = hk // heads
    pad = 128
    assert s % ts == 0 and ts % BLOCK == 0
    w_in_b = w_in.astype(BF16)
    wb = jnp.pad(w_in_b[:, conv_ch + hv:conv_ch + hv + heads], ((0, 0), (0, pad - heads)))
    wa = jnp.pad(w_in_b[:, conv_ch + hv + heads:], ((0, 0), (0, pad - heads)))
    alog = jnp.pad(a_log.reshape(1, heads), ((0, 0), (0, pad - heads)))
    dtb = jnp.pad(dt_bias.reshape(1, heads), ((0, 0), (0, pad - heads)))
    full = _resident
    kern = functools.partial(_gdn_kernel, ts=ts, heads=heads, dk=dk, dv=dv, alpha=alpha)
    return pl.pallas_call(
        kern,
        grid=(s // ts,),
        in_specs=[
            pl.BlockSpec((ts, d), lambda i: (i, 0)),
            full(d, w_in.shape[1]), full(d, pad), full(d, pad),
            full(conv_w.shape[0], conv_ch), full(1, pad), full(1, pad), full(1, dv),
            full(hv, d), full(1, d), full(1, d),
        ],
        out_specs=pl.BlockSpec((ts, d), lambda i: (i, 0)),
        out_shape=jax.ShapeDtypeStruct((s, d), F32),
        scratch_shapes=[
            pltpu.VMEM((heads, dk, dv), F32),
            pltpu.VMEM((HALO, conv_ch), F32),
            pltpu.VMEM((2, ts + HALO, 512), F32),
            pltpu.VMEM((ts, conv_ch), F32),
            pltpu.VMEM((ts, hv), F32),
            pltpu.VMEM((ts, pad), F32),
            pltpu.VMEM((ts, pad), F32),
            pltpu.VMEM((heads, BLOCK, BLOCK), F32),
            pltpu.VMEM((heads, BLOCK, BLOCK), BF16),
            pltpu.VMEM((heads, BLOCK, dv + dk), BF16),
            pltpu.VMEM((heads, BLOCK, dv), F32),
            pltpu.VMEM((heads, BLOCK, dk), BF16),
            pltpu.VMEM((heads, BLOCK, BLOCK), BF16),
            pltpu.VMEM((heads, BLOCK, dk), BF16),
            pltpu.VMEM((heads, BLOCK, dk), BF16),
            pltpu.VMEM((heads, BLOCK, dv), BF16),
            pltpu.VMEM((heads, BLOCK, dv), F32),
            pltpu.VMEM((ts, hv), BF16),
        ],
        compiler_params=pltpu.CompilerParams(
            dimension_semantics=("arbitrary",), vmem_limit_bytes=VMEM_LIMIT),
        name="gdn_mixer",
    )(h, w_in_b, wb, wa, conv_w, alog, dtb, norm_w.reshape(1, dv), w_out.astype(BF16),
      ln_g.reshape(1, d), ln_b.reshape(1, d))


def kernel(x, gla_w_in, gla_w_gk2, gla_b_gk, gla_norm_w, gla_w_out, gdn_w_in, gdn_conv_w, gdn_a_log, gdn_dt_bias, gdn_norm_w, gdn_w_out, sg_w_in, sg_ln_g, sg_ln_b, sg_w_sp, sg_b_sp, sg_w_out, ffn_w_in, ffn_conv_w, ffn_w_out, ln_g, ln_b):
    bsz, s, d = x.shape
    depth = ffn_w_in.shape[0]
    alpha = float((2 * depth) ** 0.25)
    n_mixers = 3
    outs = []
    for bi in range(bsz):
        h = x.reshape(s, d) if bsz == 1 else x[bi]
        for i in range(depth):
            mixer, j = i % n_mixers, i // n_mixers
            if mixer == 0:
                h = _gla_layer(h, gla_w_in[j], gla_w_gk2[j], gla_b_gk[j], gla_norm_w[j], gla_w_out[j],
                               ln_g[i, 0], ln_b[i, 0], alpha=alpha)
            elif mixer == 1:
                h = _gdn_layer(h, gdn_w_in[j], gdn_conv_w[j], gdn_a_log[j], gdn_dt_bias[j],
                               gdn_norm_w[j], gdn_w_out[j], ln_g[i, 0], ln_b[i, 0], alpha=alpha)
            else:
                h = _sgu_layer(h, sg_w_in[j], sg_ln_g[j], sg_ln_b[j], sg_w_sp[j], sg_b_sp[j],
                               sg_w_out[j], ln_g[i, 0], ln_b[i, 0], alpha=alpha)
            h = _ffn_layer(h, ffn_w_in[i], ffn_conv_w[i], ffn_w_out[i], ln_g[i, 1], ln_b[i, 1],
                           alpha=alpha)
        outs.append(h)
    return outs[0].reshape(1, s, d) if bsz == 1 else jnp.stack(outs, axis=0)
```

```python
import functools

import jax
import jax.numpy as jnp
from jax import lax
from jax.experimental import pallas as pl
from jax.experimental.pallas import tpu as pltpu

F32 = jnp.float32
BF16 = jnp.bfloat16

LN_EPS = 1e-5
RMS_EPS = 1e-6
GLA_NORMALIZER = 16.0
CHUNK = 64
BLOCK = 256
SG_CHUNK = 128
SUBLANES = 8
LANES = 128
VMEM_LIMIT = 56 * 1024 * 1024


def _mm_general(a, b, dims):
    return lax.dot_general(a, b, (dims, ((), ())), preferred_element_type=F32)


def _mm(a, b):
    return _mm_general(a, b, ((1,), (0,)))


def _dot(a, b):
    return _mm(a.astype(BF16), b.astype(BF16))


def _dot_nt(a, b):
    return _mm_general(a.astype(BF16), b.astype(BF16), ((1,), (1,)))


def _dot_tn(a, b):
    return _mm_general(a.astype(BF16), b.astype(BF16), ((0,), (0,)))


def _dot_split(m_bf16, x):
    hi = x.astype(BF16)
    r1 = x - hi.astype(F32)
    mid = r1.astype(BF16)
    lo = (r1 - mid.astype(F32)).astype(BF16)
    return _mm(m_bf16, hi) + _mm(m_bf16, mid) + _mm(m_bf16, lo)


def _layer_norm(r, g, b):
    mu = jnp.mean(r, -1, keepdims=True)
    c = r - mu
    var = jnp.mean(c * c, -1, keepdims=True)
    return c * lax.rsqrt(var + LN_EPS) * g + b


def _gelu(x):
    return 0.5 * x * (1.0 + lax.erf(x * (2.0 ** -0.5)))


def _silu(x):
    h = 0.5 * x
    return h + h * jnp.tanh(h)


def _sigmoid(x):
    return 0.5 + 0.5 * jnp.tanh(0.5 * x)


def _softplus(x):
    return jnp.maximum(x, 0.0) + jnp.log(1.0 + jnp.exp(-jnp.abs(x)))


def _block_masks():
    row = lax.broadcasted_iota(jnp.int32, (BLOCK, BLOCK), 0)
    col = lax.broadcasted_iota(jnp.int32, (BLOCK, BLOCK), 1)
    same = (row // CHUNK) == (col // CHUNK)
    incl = same & (col <= row)
    strict = same & (col < row)
    return incl, strict


def _per_chunk_row(x, r):
    parts = []
    for c in range(BLOCK // CHUNK):
        parts.append(jnp.broadcast_to(x[c * CHUNK + r:c * CHUNK + r + 1, :], (CHUNK, x.shape[1])))
    return jnp.concatenate(parts, axis=0)


def _restride(stage_ref, slot, val, starts, stride):
    tiles = val.shape[1] // LANES
    for c in range(tiles):
        stage_ref[slot, c] = val[:, c * LANES:(c + 1) * LANES]
    rows = []
    for st in starts:
        rows.append(jnp.concatenate(
            [stage_ref[slot, c, pl.ds(st, SUBLANES, stride=stride), :] for c in range(tiles)], axis=1))
    return jnp.concatenate(rows, axis=0)


def _to_interleaved(stage_ref, slot, val):
    n = val.shape[0] // SUBLANES
    return _restride(stage_ref, slot, val, range(n), n)


def _from_interleaved(stage_ref, slot, val):
    n = val.shape[0] // SUBLANES
    starts = [SUBLANES * ((SUBLANES * j) % n) + (SUBLANES * j) // n for j in range(n)]
    return _restride(stage_ref, slot, val, starts, SUBLANES)


def _project_interleaved(buf_ref, slot, carry_ref, xb, w_ref, col0, width, ts, taps):
    hist = SUBLANES * (taps - 1)
    buf_ref[slot, hist:hist + ts, :] = _mm(xb, w_ref[:, col0:col0 + width])
    first = lax.broadcasted_iota(jnp.int32, (SUBLANES, width), 0) == 0
    for m in range(1, taps):
        at = hist - SUBLANES * m
        cur = buf_ref[slot, at + ts:at + ts + SUBLANES, :]
        prev = carry_ref[at:at + SUBLANES, col0:col0 + width]
        buf_ref[slot, at:at + SUBLANES, :] = jnp.where(
            first, pltpu.roll(prev, 1, axis=0), pltpu.roll(cur, 1, axis=0))
    carry_ref[:, col0:col0 + width] = buf_ref[slot, ts:ts + hist, :]


def _conv_interleaved(buf_ref, slot, cw, ts):
    out = cw[0:1, :] * buf_ref[slot, 0:ts, :]
    for t in range(1, cw.shape[0]):
        out = out + cw[t:t + 1, :] * buf_ref[slot, SUBLANES * t:SUBLANES * t + ts, :]
    return out


def _ffn_kernel(x_ref, wi_ref, cw_ref, wo_ref, g_ref, b_ref, o_ref, hg_s, hu_s, carry_s, act_s,
                stage_s, *, ts, sub, tf, nf, alpha):
    @pl.when(pl.program_id(0) == 0)
    def _():
        carry_s[...] = jnp.zeros_like(carry_s)

    dff = nf * tf
    taps = cw_ref.shape[0]
    for u in range(ts // sub):
        rows = slice(u * sub, (u + 1) * sub)
        x = _to_interleaved(stage_s, u, x_ref[rows, :])
        xb = x.astype(BF16)
        for j in range(nf):
            slot = 2 * u + j % 2
            _project_interleaved(hg_s, slot, carry_s, xb, wi_ref, j * tf, tf, sub, taps)
            _project_interleaved(hu_s, slot, carry_s, xb, wi_ref, dff + j * tf, tf, sub, taps)
            gate = _conv_interleaved(hg_s, slot, cw_ref[:, j * tf:(j + 1) * tf], sub)
            up = _conv_interleaved(hu_s, slot, 0.5 * cw_ref[:, dff + j * tf:dff + (j + 1) * tf], sub)
            act = gate * (1.0 + lax.erf(gate * (2.0 ** -0.5))) * up
            act_s[u, :, j * tf:(j + 1) * tf] = act.astype(BF16)
        y = _mm(act_s[u], wo_ref[...])
        o_ref[rows, :] = _from_interleaved(
            stage_s, u, _layer_norm(alpha * x + y, g_ref[...], b_ref[...]))


def _resident(*shape):
    return pl.BlockSpec(shape, lambda i: (0,) * len(shape), pipeline_mode=pl.Buffered(1))


def _resident_layer(layer, *shape):
    return pl.BlockSpec((None,) + shape, lambda i: (layer,) + (0,) * len(shape),
                        pipeline_mode=pl.Buffered(1))


def _ffn_layer(h, w_in_all, conv_w, w_out_all, ln_g, ln_b, *, layer, alpha, ts=512, sub=512, tf=256):
    s, d = h.shape
    dff = w_out_all.shape[1]
    nf = dff // tf
    nsub = ts // sub
    assert s % ts == 0 and ts % sub == 0 and dff % tf == 0 and sub % (SUBLANES * SUBLANES) == 0
    hist = SUBLANES * (conv_w.shape[0] - 1)
    kern = functools.partial(_ffn_kernel, ts=ts, sub=sub, tf=tf, nf=nf, alpha=alpha)
    return pl.pallas_call(
        kern,
        grid=(s // ts,),
        in_specs=[
            pl.BlockSpec((ts, d), lambda i: (i, 0)),
            _resident_layer(layer, d, 2 * dff), _resident(conv_w.shape[0], 2 * dff),
            _resident_layer(layer, dff, d),
            _resident(1, d), _resident(1, d),
        ],
        out_specs=pl.BlockSpec((ts, d), lambda i: (i, 0)),
        out_shape=jax.ShapeDtypeStruct((s, d), F32),
        scratch_shapes=[
            pltpu.VMEM((2 * nsub, sub + hist, tf), F32),
            pltpu.VMEM((2 * nsub, sub + hist, tf), F32),
            pltpu.VMEM((hist, 2 * dff), F32),
            pltpu.VMEM((nsub, sub, dff), BF16),
            pltpu.VMEM((nsub, d // LANES, sub, LANES), F32),
        ],
        compiler_params=pltpu.CompilerParams(
            dimension_semantics=("arbitrary",), vmem_limit_bytes=VMEM_LIMIT),
        name="conv_ffn",
    )(h, w_in_all, conv_w, w_out_all, ln_g.reshape(1, d), ln_b.reshape(1, d))


def _sgu_kernel(x_ref, wu_ref, wv_ref, lg_ref, lb_ref, wsp_ref, bsp_ref, wo_ref, g_ref, b_ref,
                o_ref, m_ref, *, ts, groups, alpha):
    x = x_ref[...]
    xb = x.astype(BF16)
    u = _gelu(_mm(xb, wu_ref[...]))
    v = _gelu(_mm(xb, wv_ref[...]))
    v = _layer_norm(v, lg_ref[...], lb_ref[...])
    gw = v.shape[1] // groups
    row = lax.broadcasted_iota(jnp.int32, (SG_CHUNK, SG_CHUNK), 0)
    col = lax.broadcasted_iota(jnp.int32, (SG_CHUNK, SG_CHUNK), 1)
    causal = col <= row
    bsp = bsp_ref[...]
    for g in range(groups):
        wg = jnp.where(causal, wsp_ref[g], 0.0).astype(BF16)
        bias = jnp.broadcast_to(bsp[:, g:g + 1], (SG_CHUNK, gw))
        for c in range(ts // SG_CHUNK):
            vv = v[c * SG_CHUNK:(c + 1) * SG_CHUNK, g * gw:(g + 1) * gw].astype(BF16)
            mixed = _mm(wg, vv) + bias
            m_ref[c * SG_CHUNK:(c + 1) * SG_CHUNK, g * gw:(g + 1) * gw] = (
                u[c * SG_CHUNK:(c + 1) * SG_CHUNK, g * gw:(g + 1) * gw] * mixed).astype(BF16)
    y = _mm(m_ref[...], wo_ref[...])
    o_ref[...] = _layer_norm(alpha * x + y, g_ref[...], b_ref[...])


def _sgu_layer(h, w_in, sg_ln_g, sg_ln_b, w_sp, b_sp, w_out, ln_g, ln_b, *, alpha, ts=512):
    s, d = h.shape
    width = w_out.shape[0]
    groups = w_sp.shape[0]
    assert s % ts == 0 and ts % SG_CHUNK == 0
    w_in_b = w_in.astype(BF16)
    full = _resident
    kern = functools.partial(_sgu_kernel, ts=ts, groups=groups, alpha=alpha)
    return pl.pallas_call(
        kern,
        grid=(s // ts,),
        in_specs=[
            pl.BlockSpec((ts, d), lambda i: (i, 0)),
            pl.BlockSpec((d, width), lambda i: (0, 0)),
            pl.BlockSpec((d, width), lambda i: (0, 1)),
            full(1, width), full(1, width),
            full(groups, SG_CHUNK, SG_CHUNK),
            full(SG_CHUNK, groups),
            full(width, d), full(1, d), full(1, d),
        ],
        out_specs=pl.BlockSpec((ts, d), lambda i: (i, 0)),
        out_shape=jax.ShapeDtypeStruct((s, d), F32),
        scratch_shapes=[pltpu.VMEM((ts, width), BF16)],
        compiler_params=pltpu.CompilerParams(
            dimension_semantics=("arbitrary",), vmem_limit_bytes=VMEM_LIMIT),
        name="sgu_mixer",
    )(h, w_in_b, w_in_b, sg_ln_g.reshape(1, width), sg_ln_b.reshape(1, width), w_sp, b_sp.T,
      w_out.astype(BF16), ln_g.reshape(1, d), ln_b.reshape(1, d))


def _gla_kernel(x_ref, w_ref, wlr_ref, wgk2_ref, bgk_ref, nw_ref, wo_ref,
                g_ref, b_ref, o_ref, s_ref, q_s, k_s, v_s, gk_s, gate_s, vb_s, oi_s, qd_s, kd_s, dec_s,
                og_s, *, ts, heads, dk, dv, alpha):
    i = pl.program_id(0)

    @pl.when(i == 0)
    def _():
        s_ref[...] = jnp.zeros_like(s_ref)

    x = x_ref[...]
    xb = x.astype(BF16)
    hk, hv = heads * dk, heads * dv
    q_s[...] = _mm(xb, w_ref[:, 0:hk]) * (dk ** -0.5)
    k_s[...] = _mm(xb, w_ref[:, hk:2 * hk])
    v_s[...] = _mm(xb, w_ref[:, 2 * hk:2 * hk + hv])
    lr = _mm(xb, wlr_ref[...])
    z = _dot(lr, wgk2_ref[...]) + bgk_ref[...]
    gk_s[...] = -_softplus(-z) * (1.0 / GLA_NORMALIZER)

    gate_s[...] = _mm(xb, w_ref[:, 2 * hk + hv:2 * hk + 2 * hv])

    incl, _ = _block_masks()
    tril = jnp.where(incl, 1.0, 0.0).astype(BF16)
    nchunk = BLOCK // CHUNK

    for blk in range(ts // BLOCK):
        r0 = blk * BLOCK
        bcum_all = _dot_split(tril, gk_s[r0:r0 + BLOCK, :])
        for h in range(heads):
            bc = bcum_all[:, h * dk:(h + 1) * dk]
            bref = _per_chunk_row(bc, CHUNK // 2)
            blast = _per_chunk_row(bc, CHUNK - 1)
            qh = q_s[r0:r0 + BLOCK, h * dk:(h + 1) * dk]
            kh = k_s[r0:r0 + BLOCK, h * dk:(h + 1) * dk]
            vh = v_s[r0:r0 + BLOCK, h * dv:(h + 1) * dv].astype(BF16)
            vb_s[h] = vh
            scores = _dot_nt(qh * jnp.exp(bc - bref), kh * jnp.exp(bref - bc))
            oi_s[h] = _dot(jnp.where(incl, scores, 0.0), vh)
            qd_s[h] = (qh * jnp.exp(bc)).astype(BF16)
            kd_s[h] = (kh * jnp.exp(blast - bc)).astype(BF16)
            dec_s[h] = jnp.exp(bc.T)

        for c in range(nchunk):
            rows = slice(c * CHUNK, (c + 1) * CHUNK)
            last = c * CHUNK + CHUNK - 1
            for h in range(heads):
                state = s_ref[h]
                oi_s[h, rows, :] += _mm(qd_s[h, rows, :], state.astype(BF16))
                d_state = _dot_tn(kd_s[h, rows, :], vb_s[h, rows, :])
                s_ref[h] = state * dec_s[h, :, last:last + 1] + d_state

        for h in range(heads):
            o = oi_s[h]
            o = o * lax.rsqrt(jnp.mean(o * o, -1, keepdims=True) + RMS_EPS) * nw_ref[...]
            gate = gate_s[r0:r0 + BLOCK, h * dv:(h + 1) * dv]
            og_s[r0:r0 + BLOCK, h * dv:(h + 1) * dv] = (o * _silu(gate)).astype(BF16)

    y = _mm(og_s[...], wo_ref[...])
    o_ref[...] = _layer_norm(alpha * x + y, g_ref[...], b_ref[...])


def _gla_layer(h, w_in, w_gk2, b_gk, norm_w, w_out, ln_g, ln_b, *, alpha, ts=512):
    s, d = h.shape
    dv = norm_w.shape[0]
    hv = w_out.shape[0]
    heads = hv // dv
    hk = w_gk2.shape[1]
    dk = hk // heads
    rank = w_gk2.shape[0]
    lr_pad = 128
    assert s % ts == 0 and ts % BLOCK == 0
    w_in_b = w_in.astype(BF16)
    wlr = jnp.pad(w_in_b[:, 2 * hk + 2 * hv:], ((0, 0), (0, lr_pad - rank)))
    wgk2 = jnp.pad(w_gk2.astype(BF16), ((0, lr_pad - rank), (0, 0)))
    full = _resident
    kern = functools.partial(_gla_kernel, ts=ts, heads=heads, dk=dk, dv=dv, alpha=alpha)
    return pl.pallas_call(
        kern,
        grid=(s // ts,),
        in_specs=[
            pl.BlockSpec((ts, d), lambda i: (i, 0)),
            full(d, w_in.shape[1]), full(d, lr_pad),
            full(lr_pad, hk), full(1, hk), full(1, dv), full(hv, d), full(1, d), full(1, d),
        ],
        out_specs=pl.BlockSpec((ts, d), lambda i: (i, 0)),
        out_shape=jax.ShapeDtypeStruct((s, d), F32),
        scratch_shapes=[
            pltpu.VMEM((heads, dk, dv), F32),
            pltpu.VMEM((ts, hk), F32),
            pltpu.VMEM((ts, hk), F32),
            pltpu.VMEM((ts, hv), F32),
            pltpu.VMEM((ts, hk), F32),
            pltpu.VMEM((ts, hv), F32),
            pltpu.VMEM((heads, BLOCK, dv), BF16),
            pltpu.VMEM((heads, BLOCK, dv), F32),
            pltpu.VMEM((heads, BLOCK, dk), BF16),
            pltpu.VMEM((heads, BLOCK, dk), BF16),
            pltpu.VMEM((heads, dk, BLOCK), F32),
            pltpu.VMEM((ts, hv), BF16),
        ],
        compiler_params=pltpu.CompilerParams(
            dimension_semantics=("arbitrary",), vmem_limit_bytes=VMEM_LIMIT),
        name="gla_mixer",
    )(h, w_in_b, wlr, wgk2, b_gk.reshape(1, hk), norm_w.reshape(1, dv),
      w_out.astype(BF16), ln_g.reshape(1, d), ln_b.reshape(1, d))


def _gdn_kernel(x_ref, w_ref, wb_ref, wa_ref, cw_ref, alog_ref, dtb_ref, nw_ref, wo_ref,
                g_ref, b_ref, o_ref, s_ref, carry_ref, pre_s, stage_s, qkv_s, gate_s, beta_s, glog_s, x_s, p_s,
                rhs_s, u_s, w_s, qk_s, qd_s, kd_s, vn_s, o_s, og_s, *, ts, heads, dk, dv, alpha):
    i = pl.program_id(0)
    hk = heads * dk

    @pl.when(i == 0)
    def _():
        s_ref[...] = jnp.zeros_like(s_ref)
        carry_ref[...] = jnp.zeros_like(carry_ref)

    x = x_ref[...]
    xb = x.astype(BF16)
    xib = _to_interleaved(stage_s, 0, x).astype(BF16)
    tc = pre_s.shape[2]
    taps = cw_ref.shape[0]
    for j in range(qkv_s.shape[1] // tc):
        slot = j % 2
        _project_interleaved(pre_s, slot, carry_ref, xib, w_ref, j * tc, tc, ts, taps)
        act = _silu(_conv_interleaved(pre_s, slot, cw_ref[:, j * tc:(j + 1) * tc], ts))
        qkv_s[:, j * tc:(j + 1) * tc] = _from_interleaved(stage_s, slot, act)

    conv_ch = qkv_s.shape[1]
    gate_s[...] = _mm(xb, w_ref[:, conv_ch:conv_ch + heads * dv])
    beta_s[...] = _sigmoid(_mm(xb, wb_ref[...]))
    a_lin = _mm(xb, wa_ref[...])
    glog_s[...] = -jnp.exp(alog_ref[...]) * _softplus(a_lin + dtb_ref[...])

    nchunk = BLOCK // CHUNK
    steps = (CHUNK - 1).bit_length() - 1

    def block_body(blk, carry):
        rows_blk = pl.ds(pl.multiple_of(blk * BLOCK, BLOCK), BLOCK)
        incl, strict = _block_masks()
        tril = jnp.where(incl, 1.0, 0.0).astype(BF16)
        row = lax.broadcasted_iota(jnp.int32, (BLOCK, BLOCK), 0)
        col = lax.broadcasted_iota(jnp.int32, (BLOCK, BLOCK), 1)
        bcum = _dot_split(tril, glog_s[rows_blk, :])
        bcum_t = bcum.T
        blast = _per_chunk_row(bcum, CHUNK - 1)
        e_b = jnp.exp(bcum)
        e_kd = jnp.exp(blast - bcum)
        beta_blk = beta_s[rows_blk, :]

        for h in range(heads):
            q = qkv_s[rows_blk, h * dk:(h + 1) * dk]
            k = qkv_s[rows_blk, hk + h * dk:hk + (h + 1) * dk]
            v = qkv_s[rows_blk, 2 * hk + h * dv:2 * hk + (h + 1) * dv]
            q = q * lax.rsqrt(jnp.sum(q * q, -1, keepdims=True) + RMS_EPS) * (dk ** -0.5)
            k = k * lax.rsqrt(jnp.sum(k * k, -1, keepdims=True) + RMS_EPS)
            beta = beta_blk[:, h:h + 1]
            diff = bcum[:, h:h + 1] - bcum_t[h:h + 1, :]
            decay = jnp.exp(jnp.where(incl, diff, -jnp.inf))
            kb = k * beta
            kbf = k.astype(BF16)
            a = jnp.where(strict, -(_dot_nt(kb, kbf) * decay), 0.0)
            p_s[h] = a.astype(BF16)
            x_s[h] = jnp.where(row == col, 1.0, a)
            rhs_s[h, :, 0:dv] = (v * beta).astype(BF16)
            rhs_s[h, :, dv:dv + dk] = (kb * e_b[:, h:h + 1]).astype(BF16)
            qk_s[h] = (_dot_nt(q, kbf) * decay).astype(BF16)
            qd_s[h] = (q * e_b[:, h:h + 1]).astype(BF16)
            kd_s[h] = (k * e_kd[:, h:h + 1]).astype(BF16)

        for step in range(steps):
            for h in range(heads):
                pb = _mm(p_s[h], p_s[h]).astype(BF16)
                if step + 1 < steps:
                    p_s[h] = pb
                xh = x_s[h]
                x_s[h] = xh + _mm(xh.astype(BF16), pb)

        for h in range(heads):
            sol = _mm(x_s[h].astype(BF16), rhs_s[h])
            u_s[h] = sol[:, :dv]
            w_s[h] = sol[:, dv:].astype(BF16)

        for c in range(nchunk):
            rows = slice(c * CHUNK, (c + 1) * CHUNK)
            last = c * CHUNK + CHUNK - 1
            for h in range(heads):
                state = s_ref[h]
                sb = state.astype(BF16)
                v_new = u_s[h, rows, :] - _mm(w_s[h, rows, :], sb)
                vb = v_new.astype(BF16)
                vn_s[h, rows, :] = vb
                o_s[h, rows, :] = _mm(qd_s[h, rows, :], sb)
                cd = jnp.exp(blast[last:last + 1, h:h + 1])
                s_ref[h] = state * cd + _dot_tn(kd_s[h, rows, :], vb)

        for h in range(heads):
            o = o_s[h] + _mm(qk_s[h], vn_s[h])
            o = o * lax.rsqrt(jnp.mean(o * o, -1, keepdims=True) + RMS_EPS) * nw_ref[...]
            gate = gate_s[rows_blk, h * dv:(h + 1) * dv]
            og_s[rows_blk, h * dv:(h + 1) * dv] = (o * _silu(gate)).astype(BF16)
        return carry

    lax.fori_loop(0, ts // BLOCK, block_body, 0)

    y = _mm(og_s[...], wo_ref[...])
    o_ref[...] = _layer_norm(alpha * x + y, g_ref[...], b_ref[...])


def _gdn_layer(h, w_in, conv_w, a_log, dt_bias, norm_w, w_out, ln_g, ln_b, *, alpha, ts=512):
    s, d = h.shape
    dv = norm_w.shape[0]
    hv = w_out.shape[0]
    heads = hv // dv
    conv_ch = conv_w.shape[1]
    hk = (conv_ch - hv) // 2
    dk = hk // heads
    pad = 128
    assert s % ts == 0 and ts % BLOCK == 0
    hist = SUBLANES * (conv_w.shape[0] - 1)
    w_in_b = w_in.astype(BF16)
    wb = jnp.pad(w_in_b[:, conv_ch + hv:conv_ch + hv + heads], ((0, 0), (0, pad - heads)))
    wa = jnp.pad(w_in_b[:, conv_ch + hv + heads:], ((0, 0), (0, pad - heads)))
    alog = jnp.pad(a_log.reshape(1, heads), ((0, 0), (0, pad - heads)))
    dtb = jnp.pad(dt_bias.reshape(1, heads), ((0, 0), (0, pad - heads)))
    full = _resident
    kern = functools.partial(_gdn_kernel, ts=ts, heads=heads, dk=dk, dv=dv, alpha=alpha)
    return pl.pallas_call(
        kern,
        grid=(s // ts,),
        in_specs=[
            pl.BlockSpec((ts, d), lambda i: (i, 0)),
            full(d, w_in.shape[1]), full(d, pad), full(d, pad),
            full(conv_w.shape[0], conv_ch), full(1, pad), full(1, pad), full(1, dv),
            full(hv, d), full(1, d), full(1, d),
        ],
        out_specs=pl.BlockSpec((ts, d), lambda i: (i, 0)),
        out_shape=jax.ShapeDtypeStruct((s, d), F32),
        scratch_shapes=[
            pltpu.VMEM((heads, dk, dv), F32),
            pltpu.VMEM((hist, conv_ch), F32),
            pltpu.VMEM((2, ts + hist, 512), F32),
            pltpu.VMEM((2, d // LANES, ts, LANES), F32),
            pltpu.VMEM((ts, conv_ch), F32),
            pltpu.VMEM((ts, hv), F32),
            pltpu.VMEM((ts, pad), F32),
            pltpu.VMEM((ts, pad), F32),
            pltpu.VMEM((heads, BLOCK, BLOCK), F32),
            pltpu.VMEM((heads, BLOCK, BLOCK), BF16),
            pltpu.VMEM((heads, BLOCK, dv + dk), BF16),
            pltpu.VMEM((heads, BLOCK, dv), F32),
            pltpu.VMEM((heads, BLOCK, dk), BF16),
            pltpu.VMEM((heads, BLOCK, BLOCK), BF16),
            pltpu.VMEM((heads, BLOCK, dk), BF16),
            pltpu.VMEM((heads, BLOCK, dk), BF16),
            pltpu.VMEM((heads, BLOCK, dv), BF16),
            pltpu.VMEM((heads, BLOCK, dv), F32),
            pltpu.VMEM((ts, hv), BF16),
        ],
        compiler_params=pltpu.CompilerParams(
            dimension_semantics=("arbitrary",), vmem_limit_bytes=VMEM_LIMIT),
        name="gdn_mixer",
    )(h, w_in_b, wb, wa, conv_w, alog, dtb, norm_w.reshape(1, dv), w_out.astype(BF16),
      ln_g.reshape(1, d), ln_b.reshape(1, d))


def kernel(x, gla_w_in, gla_w_gk2, gla_b_gk, gla_norm_w, gla_w_out, gdn_w_in, gdn_conv_w, gdn_a_log, gdn_dt_bias, gdn_norm_w, gdn_w_out, sg_w_in, sg_ln_g, sg_ln_b, sg_w_sp, sg_b_sp, sg_w_out, ffn_w_in, ffn_conv_w, ffn_w_out, ln_g, ln_b):
    bsz, s, d = x.shape
    depth = ffn_w_in.shape[0]
    alpha = float((2 * depth) ** 0.25)
    n_mixers = 3
    ffn_w_in_b = ffn_w_in.astype(BF16)
    ffn_w_out_b = ffn_w_out.astype(BF16)
    outs = []
    for bi in range(bsz):
        h = x.reshape(s, d) if bsz == 1 else x[bi]
        for i in range(depth):
            mixer, j = i % n_mixers, i // n_mixers
            if mixer == 0:
                h = _gla_layer(h, gla_w_in[j], gla_w_gk2[j], gla_b_gk[j], gla_norm_w[j], gla_w_out[j],
                               ln_g[i, 0], ln_b[i, 0], alpha=alpha)
            elif mixer == 1:
                h = _gdn_layer(h, gdn_w_in[j], gdn_conv_w[j], gdn_a_log[j], gdn_dt_bias[j],
                               gdn_norm_w[j], gdn_w_out[j], ln_g[i, 0], ln_b[i, 0], alpha=alpha)
            else:
                h = _sgu_layer(h, sg_w_in[j], sg_ln_g[j], sg_ln_b[j], sg_w_sp[j], sg_b_sp[j],
                               sg_w_out[j], ln_g[i, 0], ln_b[i, 0], alpha=alpha)
            h = _ffn_layer(h, ffn_w_in_b, ffn_conv_w[i], ffn_w_out_b, ln_g[i, 1], ln_b[i, 1],
                           layer=i, alpha=alpha)
        outs.append(h)
    return outs[0].reshape(1, s, d) if bsz == 1 else jnp.stack(outs, axis=0)
```

```python
import functools

import jax
import jax.numpy as jnp
from jax import lax
from jax.experimental import pallas as pl
from jax.experimental.pallas import tpu as pltpu

F32 = jnp.float32
BF16 = jnp.bfloat16

LN_EPS = 1e-5
RMS_EPS = 1e-6
GLA_NORMALIZER = 16.0
CHUNK = 64
BLOCK = 256
SG_CHUNK = 128
SUBLANES = 8
LANES = 128
VMEM_LIMIT = 56 * 1024 * 1024


def _mm_general(a, b, dims):
    return lax.dot_general(a, b, (dims, ((), ())), preferred_element_type=F32)


def _mm(a, b):
    return _mm_general(a, b, ((1,), (0,)))


def _dot(a, b):
    return _mm(a.astype(BF16), b.astype(BF16))


def _dot_nt(a, b):
    return _mm_general(a.astype(BF16), b.astype(BF16), ((1,), (1,)))


def _dot_tn(a, b):
    return _mm_general(a.astype(BF16), b.astype(BF16), ((0,), (0,)))


def _dot_split(m_bf16, x):
    hi = x.astype(BF16)
    r1 = x - hi.astype(F32)
    mid = r1.astype(BF16)
    lo = (r1 - mid.astype(F32)).astype(BF16)
    return _mm(m_bf16, hi) + _mm(m_bf16, mid) + _mm(m_bf16, lo)


def _layer_norm(r, g, b):
    mu = jnp.mean(r, -1, keepdims=True)
    c = r - mu
    var = jnp.mean(c * c, -1, keepdims=True)
    return c * lax.rsqrt(var + LN_EPS) * g + b


def _gelu(x):
    return 0.5 * x * (1.0 + lax.erf(x * (2.0 ** -0.5)))


def _silu(x):
    h = 0.5 * x
    return h + h * jnp.tanh(h)


def _sigmoid(x):
    return 0.5 + 0.5 * jnp.tanh(0.5 * x)


def _softplus(x):
    return jnp.maximum(x, 0.0) + jnp.log(1.0 + jnp.exp(-jnp.abs(x)))


def _block_masks():
    row = lax.broadcasted_iota(jnp.int32, (BLOCK, BLOCK), 0)
    col = lax.broadcasted_iota(jnp.int32, (BLOCK, BLOCK), 1)
    same = (row // CHUNK) == (col // CHUNK)
    incl = same & (col <= row)
    strict = same & (col < row)
    return incl, strict


def _per_chunk_row(x, r):
    parts = []
    for c in range(BLOCK // CHUNK):
        parts.append(jnp.broadcast_to(x[c * CHUNK + r:c * CHUNK + r + 1, :], (CHUNK, x.shape[1])))
    return jnp.concatenate(parts, axis=0)


def _restride(stage_ref, slot, val, starts, stride):
    tiles = val.shape[1] // LANES
    for c in range(tiles):
        stage_ref[slot, c] = val[:, c * LANES:(c + 1) * LANES]
    rows = []
    for st in starts:
        rows.append(jnp.concatenate(
            [stage_ref[slot, c, pl.ds(st, SUBLANES, stride=stride), :] for c in range(tiles)], axis=1))
    return jnp.concatenate(rows, axis=0)


def _to_interleaved(stage_ref, slot, val):
    n = val.shape[0] // SUBLANES
    return _restride(stage_ref, slot, val, range(n), n)


def _from_interleaved(stage_ref, slot, val):
    n = val.shape[0] // SUBLANES
    starts = [SUBLANES * ((SUBLANES * j) % n) + (SUBLANES * j) // n for j in range(n)]
    return _restride(stage_ref, slot, val, starts, SUBLANES)


def _project_interleaved(buf_ref, slot, carry_ref, xb, w_ref, col0, width, ts, taps):
    hist = SUBLANES * (taps - 1)
    buf_ref[slot, hist:hist + ts, :] = _mm(xb, w_ref[:, col0:col0 + width])
    first = lax.broadcasted_iota(jnp.int32, (SUBLANES, width), 0) == 0
    for m in range(1, taps):
        at = hist - SUBLANES * m
        cur = buf_ref[slot, at + ts:at + ts + SUBLANES, :]
        prev = carry_ref[at:at + SUBLANES, col0:col0 + width]
        buf_ref[slot, at:at + SUBLANES, :] = jnp.where(
            first, pltpu.roll(prev, 1, axis=0), pltpu.roll(cur, 1, axis=0))
    carry_ref[:, col0:col0 + width] = buf_ref[slot, ts:ts + hist, :]


def _conv_interleaved(buf_ref, slot, cw, ts):
    out = cw[0:1, :] * buf_ref[slot, 0:ts, :]
    for t in range(1, cw.shape[0]):
        out = out + cw[t:t + 1, :] * buf_ref[slot, SUBLANES * t:SUBLANES * t + ts, :]
    return out


def _ffn_kernel(x_ref, wi_ref, cw_ref, wo_ref, g_ref, b_ref, o_ref, hg_s, hu_s, carry_s, act_s,
                stage_s, *, ts, sub, tf, nf, alpha):
    @pl.when(pl.program_id(0) == 0)
    def _():
        carry_s[...] = jnp.zeros_like(carry_s)

    dff = nf * tf
    taps = cw_ref.shape[0]
    for u in range(ts // sub):
        rows = slice(u * sub, (u + 1) * sub)
        x = _to_interleaved(stage_s, u, x_ref[rows, :])
        xb = x.astype(BF16)
        for j in range(nf):
            slot = 2 * u + j % 2
            _project_interleaved(hg_s, slot, carry_s, xb, wi_ref, j * tf, tf, sub, taps)
            _project_interleaved(hu_s, slot, carry_s, xb, wi_ref, dff + j * tf, tf, sub, taps)
            gate = _conv_interleaved(hg_s, slot, cw_ref[:, j * tf:(j + 1) * tf], sub)
            up = _conv_interleaved(hu_s, slot, 0.5 * cw_ref[:, dff + j * tf:dff + (j + 1) * tf], sub)
            act = gate * (1.0 + lax.erf(gate * (2.0 ** -0.5))) * up
            act_s[u, :, j * tf:(j + 1) * tf] = act.astype(BF16)
        y = _mm(act_s[u], wo_ref[...])
        o_ref[rows, :] = _from_interleaved(
            stage_s, u, _layer_norm(alpha * x + y, g_ref[...], b_ref[...]))


def _resident(*shape):
    return pl.BlockSpec(shape, lambda i: (0,) * len(shape), pipeline_mode=pl.Buffered(1))


def _resident_layer(layer, *shape):
    return pl.BlockSpec((None,) + shape, lambda i: (layer,) + (0,) * len(shape),
                        pipeline_mode=pl.Buffered(1))


def _ffn_layer(h, w_in_all, conv_w, w_out_all, ln_g, ln_b, *, layer, alpha, ts=512, sub=512, tf=256):
    s, d = h.shape
    dff = w_out_all.shape[1]
    nf = dff // tf
    nsub = ts // sub
    assert s % ts == 0 and ts % sub == 0 and dff % tf == 0 and sub % (SUBLANES * SUBLANES) == 0
    hist = SUBLANES * (conv_w.shape[0] - 1)
    kern = functools.partial(_ffn_kernel, ts=ts, sub=sub, tf=tf, nf=nf, alpha=alpha)
    return pl.pallas_call(
        kern,
        grid=(s // ts,),
        in_specs=[
            pl.BlockSpec((ts, d), lambda i: (i, 0)),
            _resident_layer(layer, d, 2 * dff), _resident(conv_w.shape[0], 2 * dff),
            _resident_layer(layer, dff, d),
            _resident(1, d), _resident(1, d),
        ],
        out_specs=pl.BlockSpec((ts, d), lambda i: (i, 0)),
        out_shape=jax.ShapeDtypeStruct((s, d), F32),
        scratch_shapes=[
            pltpu.VMEM((2 * nsub, sub + hist, tf), F32),
            pltpu.VMEM((2 * nsub, sub + hist, tf), F32),
            pltpu.VMEM((hist, 2 * dff), F32),
            pltpu.VMEM((nsub, sub, dff), BF16),
            pltpu.VMEM((nsub, d // LANES, sub, LANES), F32),
        ],
        compiler_params=pltpu.CompilerParams(
            dimension_semantics=("arbitrary",), vmem_limit_bytes=VMEM_LIMIT),
        name="conv_ffn",
    )(h, w_in_all, conv_w, w_out_all, ln_g.reshape(1, d), ln_b.reshape(1, d))


def _sgu_kernel(x_ref, wu_ref, wv_ref, lg_ref, lb_ref, wsp_ref, bsp_ref, wo_ref, g_ref, b_ref,
                o_ref, m_ref, *, ts, groups, alpha):
    x = x_ref[...]
    xb = x.astype(BF16)
    u = _gelu(_mm(xb, wu_ref[...]))
    v = _gelu(_mm(xb, wv_ref[...]))
    v = _layer_norm(v, lg_ref[...], lb_ref[...])
    gw = v.shape[1] // groups
    row = lax.broadcasted_iota(jnp.int32, (SG_CHUNK, SG_CHUNK), 0)
    col = lax.broadcasted_iota(jnp.int32, (SG_CHUNK, SG_CHUNK), 1)
    causal = col <= row
    bsp = bsp_ref[...]
    for g in range(groups):
        wg = jnp.where(causal, wsp_ref[g], 0.0).astype(BF16)
        bias = jnp.broadcast_to(bsp[:, g:g + 1], (SG_CHUNK, gw))
        for c in range(ts // SG_CHUNK):
            vv = v[c * SG_CHUNK:(c + 1) * SG_CHUNK, g * gw:(g + 1) * gw].astype(BF16)
            mixed = _mm(wg, vv) + bias
            m_ref[c * SG_CHUNK:(c + 1) * SG_CHUNK, g * gw:(g + 1) * gw] = (
                u[c * SG_CHUNK:(c + 1) * SG_CHUNK, g * gw:(g + 1) * gw] * mixed).astype(BF16)
    y = _mm(m_ref[...], wo_ref[...])
    o_ref[...] = _layer_norm(alpha * x + y, g_ref[...], b_ref[...])


def _sgu_layer(h, w_in, sg_ln_g, sg_ln_b, w_sp, b_sp, w_out, ln_g, ln_b, *, alpha, ts=512):
    s, d = h.shape
    width = w_out.shape[0]
    groups = w_sp.shape[0]
    assert s % ts == 0 and ts % SG_CHUNK == 0
    w_in_b = w_in.astype(BF16)
    full = _resident
    kern = functools.partial(_sgu_kernel, ts=ts, groups=groups, alpha=alpha)
    return pl.pallas_call(
        kern,
        grid=(s // ts,),
        in_specs=[
            pl.BlockSpec((ts, d), lambda i: (i, 0)),
            pl.BlockSpec((d, width), lambda i: (0, 0)),
            pl.BlockSpec((d, width), lambda i: (0, 1)),
            full(1, width), full(1, width),
            full(groups, SG_CHUNK, SG_CHUNK),
            full(SG_CHUNK, groups),
            full(width, d), full(1, d), full(1, d),
        ],
        out_specs=pl.BlockSpec((ts, d), lambda i: (i, 0)),
        out_shape=jax.ShapeDtypeStruct((s, d), F32),
        scratch_shapes=[pltpu.VMEM((ts, width), BF16)],
        compiler_params=pltpu.CompilerParams(
            dimension_semantics=("arbitrary",), vmem_limit_bytes=VMEM_LIMIT),
        name="sgu_mixer",
    )(h, w_in_b, w_in_b, sg_ln_g.reshape(1, width), sg_ln_b.reshape(1, width), w_sp, b_sp.T,
      w_out.astype(BF16), ln_g.reshape(1, d), ln_b.reshape(1, d))


def _gla_kernel(x_ref, w_ref, wlr_ref, wgk2_ref, bgk_ref, nw_ref, wo_ref,
                g_ref, b_ref, o_ref, s_ref, q_s, k_s, v_s, gk_s, gate_s, vb_s, oi_s, qd_s, kd_s, dec_s,
                og_s, *, ts, heads, dk, dv, alpha):
    i = pl.program_id(0)

    @pl.when(i == 0)
    def _():
        s_ref[...] = jnp.zeros_like(s_ref)

    x = x_ref[...]
    xb = x.astype(BF16)
    hk, hv = heads * dk, heads * dv
    q_s[...] = _mm(xb, w_ref[:, 0:hk]) * (dk ** -0.5)
    k_s[...] = _mm(xb, w_ref[:, hk:2 * hk])
    v_s[...] = _mm(xb, w_ref[:, 2 * hk:2 * hk + hv])
    lr = _mm(xb, wlr_ref[...])
    z = _dot(lr, wgk2_ref[...]) + bgk_ref[...]
    gk_s[...] = -_softplus(-z) * (1.0 / GLA_NORMALIZER)

    gate_s[...] = _mm(xb, w_ref[:, 2 * hk + hv:2 * hk + 2 * hv])

    incl, _ = _block_masks()
    tril = jnp.where(incl, 1.0, 0.0).astype(BF16)
    nchunk = BLOCK // CHUNK

    for blk in range(ts // BLOCK):
        r0 = blk * BLOCK
        bcum_all = _dot_split(tril, gk_s[r0:r0 + BLOCK, :])
        for h in range(heads):
            bc = bcum_all[:, h * dk:(h + 1) * dk]
            bref = _per_chunk_row(bc, CHUNK // 2)
            blast = _per_chunk_row(bc, CHUNK - 1)
            qh = q_s[r0:r0 + BLOCK, h * dk:(h + 1) * dk]
            kh = k_s[r0:r0 + BLOCK, h * dk:(h + 1) * dk]
            vh = v_s[r0:r0 + BLOCK, h * dv:(h + 1) * dv].astype(BF16)
            vb_s[h] = vh
            scores = _dot_nt(qh * jnp.exp(bc - bref), kh * jnp.exp(bref - bc))
            oi_s[h] = _dot(jnp.where(incl, scores, 0.0), vh)
            qd_s[h] = (qh * jnp.exp(bc)).astype(BF16)
            kd_s[h] = (kh * jnp.exp(blast - bc)).astype(BF16)
            dec_s[h] = jnp.exp(bc.T)

        for c in range(nchunk):
            rows = slice(c * CHUNK, (c + 1) * CHUNK)
            last = c * CHUNK + CHUNK - 1
            for h in range(heads):
                state = s_ref[h]
                oi_s[h, rows, :] += _mm(qd_s[h, rows, :], state.astype(BF16))
                d_state = _dot_tn(kd_s[h, rows, :], vb_s[h, rows, :])
                s_ref[h] = state * dec_s[h, :, last:last + 1] + d_state

        for h in range(heads):
            o = oi_s[h]
            o = o * lax.rsqrt(jnp.mean(o * o, -1, keepdims=True) + RMS_EPS) * nw_ref[...]
            gate = gate_s[r0:r0 + BLOCK, h * dv:(h + 1) * dv]
            og_s[r0:r0 + BLOCK, h * dv:(h + 1) * dv] = (o * _silu(gate)).astype(BF16)

    y = _mm(og_s[...], wo_ref[...])
    o_ref[...] = _layer_norm(alpha * x + y, g_ref[...], b_ref[...])


def _gla_layer(h, w_in, w_gk2, b_gk, norm_w, w_out, ln_g, ln_b, *, alpha, ts=512):
    s, d = h.shape
    dv = norm_w.shape[0]
    hv = w_out.shape[0]
    heads = hv // dv
    hk = w_gk2.shape[1]
    dk = hk // heads
    rank = w_gk2.shape[0]
    lr_pad = 128
    assert s % ts == 0 and ts % BLOCK == 0
    w_in_b = w_in.astype(BF16)
    wlr = jnp.pad(w_in_b[:, 2 * hk + 2 * hv:], ((0, 0), (0, lr_pad - rank)))
    wgk2 = jnp.pad(w_gk2.astype(BF16), ((0, lr_pad - rank), (0, 0)))
    full = _resident
    kern = functools.partial(_gla_kernel, ts=ts, heads=heads, dk=dk, dv=dv, alpha=alpha)
    return pl.pallas_call(
        kern,
        grid=(s // ts,),
        in_specs=[
            pl.BlockSpec((ts, d), lambda i: (i, 0)),
            full(d, w_in.shape[1]), full(d, lr_pad),
            full(lr_pad, hk), full(1, hk), full(1, dv), full(hv, d), full(1, d), full(1, d),
        ],
        out_specs=pl.BlockSpec((ts, d), lambda i: (i, 0)),
        out_shape=jax.ShapeDtypeStruct((s, d), F32),
        scratch_shapes=[
            pltpu.VMEM((heads, dk, dv), F32),
            pltpu.VMEM((ts, hk), F32),
            pltpu.VMEM((ts, hk), F32),
            pltpu.VMEM((ts, hv), F32),
            pltpu.VMEM((ts, hk), F32),
            pltpu.VMEM((ts, hv), F32),
            pltpu.VMEM((heads, BLOCK, dv), BF16),
            pltpu.VMEM((heads, BLOCK, dv), F32),
            pltpu.VMEM((heads, BLOCK, dk), BF16),
            pltpu.VMEM((heads, BLOCK, dk), BF16),
            pltpu.VMEM((heads, dk, BLOCK), F32),
            pltpu.VMEM((ts, hv), BF16),
        ],
        compiler_params=pltpu.CompilerParams(
            dimension_semantics=("arbitrary",), vmem_limit_bytes=VMEM_LIMIT),
        name="gla_mixer",
    )(h, w_in_b, wlr, wgk2, b_gk.reshape(1, hk), norm_w.reshape(1, dv),
      w_out.astype(BF16), ln_g.reshape(1, d), ln_b.reshape(1, d))


def _gdn_kernel(x_ref, w_ref, wb_ref, wa_ref, cw_ref, alog_ref, dtb_ref, nw_ref, wo_ref,
                g_ref, b_ref, o_ref, s_ref, carry_ref, pre_s, stage_s, qkv_s, gate_s, beta_s, glog_s, x_s, p_s,
                rhs_s, u_s, w_s, qk_s, qd_s, kd_s, vn_s, o_s, og_s, *, ts, heads, dk, dv, alpha):
    i = pl.program_id(0)
    hk = heads * dk

    @pl.when(i == 0)
    def _():
        s_ref[...] = jnp.zeros_like(s_ref)
        carry_ref[...] = jnp.zeros_like(carry_ref)

    x = x_ref[...]
    xb = x.astype(BF16)
    xib = _to_interleaved(stage_s, 0, x).astype(BF16)
    tc = pre_s.shape[2]
    taps = cw_ref.shape[0]
    for j in range(qkv_s.shape[1] // tc):
        slot = j % 2
        _project_interleaved(pre_s, slot, carry_ref, xib, w_ref, j * tc, tc, ts, taps)
        act = _silu(_conv_interleaved(pre_s, slot, cw_ref[:, j * tc:(j + 1) * tc], ts))
        qkv_s[:, j * tc:(j + 1) * tc] = _from_interleaved(stage_s, slot, act)

    conv_ch = qkv_s.shape[1]
    gate_s[...] = _mm(xb, w_ref[:, conv_ch:conv_ch + heads * dv])
    beta_s[...] = _sigmoid(_mm(xb, wb_ref[...]))
    a_lin = _mm(xb, wa_ref[...])
    glog_s[...] = -jnp.exp(alog_ref[...]) * _softplus(a_lin + dtb_ref[...])

    nchunk = BLOCK // CHUNK
    steps = (CHUNK - 1).bit_length() - 1

    def block_body(blk):
        rows_blk = pl.ds(blk * BLOCK, BLOCK)

        def hb(h):
            return blk * heads + h

        incl, strict = _block_masks()
        tril = jnp.where(incl, 1.0, 0.0).astype(BF16)
        row = lax.broadcasted_iota(jnp.int32, (BLOCK, BLOCK), 0)
        col = lax.broadcasted_iota(jnp.int32, (BLOCK, BLOCK), 1)
        bcum = _dot_split(tril, glog_s[rows_blk, :])
        bcum_t = bcum.T
        blast = _per_chunk_row(bcum, CHUNK - 1)
        e_b = jnp.exp(bcum)
        e_kd = jnp.exp(blast - bcum)
        beta_blk = beta_s[rows_blk, :]

        for h in range(heads):
            q = qkv_s[rows_blk, h * dk:(h + 1) * dk]
            k = qkv_s[rows_blk, hk + h * dk:hk + (h + 1) * dk]
            v = qkv_s[rows_blk, 2 * hk + h * dv:2 * hk + (h + 1) * dv]
            q = q * lax.rsqrt(jnp.sum(q * q, -1, keepdims=True) + RMS_EPS) * (dk ** -0.5)
            k = k * lax.rsqrt(jnp.sum(k * k, -1, keepdims=True) + RMS_EPS)
            beta = beta_blk[:, h:h + 1]
            diff = bcum[:, h:h + 1] - bcum_t[h:h + 1, :]
            decay = jnp.exp(jnp.where(incl, diff, -jnp.inf))
            kb = k * beta
            kbf = k.astype(BF16)
            a = jnp.where(strict, -(_dot_nt(kb, kbf) * decay), 0.0)
            p_s[hb(h)] = a.astype(BF16)
            x_s[hb(h)] = jnp.where(row == col, 1.0, a)
            rhs_s[hb(h), :, 0:dv] = (v * beta).astype(BF16)
            rhs_s[hb(h), :, dv:dv + dk] = (kb * e_b[:, h:h + 1]).astype(BF16)
            qk_s[hb(h)] = (_dot_nt(q, kbf) * decay).astype(BF16)
            qd_s[hb(h)] = (q * e_b[:, h:h + 1]).astype(BF16)
            kd_s[hb(h)] = (k * e_kd[:, h:h + 1]).astype(BF16)

        for step in range(steps):
            for h in range(heads):
                pb = _mm(p_s[hb(h)], p_s[hb(h)]).astype(BF16)
                if step + 1 < steps:
                    p_s[hb(h)] = pb
                xh = x_s[hb(h)]
                x_s[hb(h)] = xh + _mm(xh.astype(BF16), pb)

        for h in range(heads):
            sol = _mm(x_s[hb(h)].astype(BF16), rhs_s[hb(h)])
            u_s[hb(h)] = sol[:, :dv]
            w_s[hb(h)] = sol[:, dv:].astype(BF16)

        for c in range(nchunk):
            rows = slice(c * CHUNK, (c + 1) * CHUNK)
            last = c * CHUNK + CHUNK - 1
            for h in range(heads):
                state = s_ref[h]
                sb = state.astype(BF16)
                v_new = u_s[hb(h), rows, :] - _mm(w_s[hb(h), rows, :], sb)
                vb = v_new.astype(BF16)
                vn_s[hb(h), rows, :] = vb
                o_s[hb(h), rows, :] = _mm(qd_s[hb(h), rows, :], sb)
                cd = jnp.exp(blast[last:last + 1, h:h + 1])
                s_ref[h] = state * cd + _dot_tn(kd_s[hb(h), rows, :], vb)

        for h in range(heads):
            o = o_s[hb(h)] + _mm(qk_s[hb(h)], vn_s[hb(h)])
            o = o * lax.rsqrt(jnp.mean(o * o, -1, keepdims=True) + RMS_EPS) * nw_ref[...]
            gate = gate_s[rows_blk, h * dv:(h + 1) * dv]
            og_s[rows_blk, h * dv:(h + 1) * dv] = (o * _silu(gate)).astype(BF16)

    for blk in range(ts // BLOCK):
        block_body(blk)

    y = _mm(og_s[...], wo_ref[...])
    o_ref[...] = _layer_norm(alpha * x + y, g_ref[...], b_ref[...])


def _gdn_layer(h, w_in, conv_w, a_log, dt_bias, norm_w, w_out, ln_g, ln_b, *, alpha, ts=512):
    s, d = h.shape
    dv = norm_w.shape[0]
    hv = w_out.shape[0]
    heads = hv // dv
    conv_ch = conv_w.shape[1]
    hk = (conv_ch - hv) // 2
    dk = hk // heads
    pad = 128
    assert s % ts == 0 and ts % BLOCK == 0
    hist = SUBLANES * (conv_w.shape[0] - 1)
    slots = (ts // BLOCK) * heads
    w_in_b = w_in.astype(BF16)
    wb = jnp.pad(w_in_b[:, conv_ch + hv:conv_ch + hv + heads], ((0, 0), (0, pad - heads)))
    wa = jnp.pad(w_in_b[:, conv_ch + hv + heads:], ((0, 0), (0, pad - heads)))
    alog = jnp.pad(a_log.reshape(1, heads), ((0, 0), (0, pad - heads)))
    dtb = jnp.pad(dt_bias.reshape(1, heads), ((0, 0), (0, pad - heads)))
    full = _resident
    kern = functools.partial(_gdn_kernel, ts=ts, heads=heads, dk=dk, dv=dv, alpha=alpha)
    return pl.pallas_call(
        kern,
        grid=(s // ts,),
        in_specs=[
            pl.BlockSpec((ts, d), lambda i: (i, 0)),
            full(d, w_in.shape[1]), full(d, pad), full(d, pad),
            full(conv_w.shape[0], conv_ch), full(1, pad), full(1, pad), full(1, dv),
            full(hv, d), full(1, d), full(1, d),
        ],
        out_specs=pl.BlockSpec((ts, d), lambda i: (i, 0)),
        out_shape=jax.ShapeDtypeStruct((s, d), F32),
        scratch_shapes=[
            pltpu.VMEM((heads, dk, dv), F32),
            pltpu.VMEM((hist, conv_ch), F32),
            pltpu.VMEM((2, ts + hist, 512), F32),
            pltpu.VMEM((2, d // LANES, ts, LANES), F32),
            pltpu.VMEM((ts, conv_ch), F32),
            pltpu.VMEM((ts, hv), F32),
            pltpu.VMEM((ts, pad), F32),
            pltpu.VMEM((ts, pad), F32),
            pltpu.VMEM((slots, BLOCK, BLOCK), F32),
            pltpu.VMEM((slots, BLOCK, BLOCK), BF16),
            pltpu.VMEM((slots, BLOCK, dv + dk), BF16),
            pltpu.VMEM((slots, BLOCK, dv), F32),
            pltpu.VMEM((slots, BLOCK, dk), BF16),
            pltpu.VMEM((slots, BLOCK, BLOCK), BF16),
            pltpu.VMEM((slots, BLOCK, dk), BF16),
            pltpu.VMEM((slots, BLOCK, dk), BF16),
            pltpu.VMEM((slots, BLOCK, dv), BF16),
            pltpu.VMEM((slots, BLOCK, dv), F32),
            pltpu.VMEM((ts, hv), BF16),
        ],
        compiler_params=pltpu.CompilerParams(
            dimension_semantics=("arbitrary",), vmem_limit_bytes=VMEM_LIMIT),
        name="gdn_mixer",
    )(h, w_in_b, wb, wa, conv_w, alog, dtb, norm_w.reshape(1, dv), w_out.astype(BF16),
      ln_g.reshape(1, d), ln_b.reshape(1, d))


def kernel(x, gla_w_in, gla_w_gk2, gla_b_gk, gla_norm_w, gla_w_out, gdn_w_in, gdn_conv_w, gdn_a_log, gdn_dt_bias, gdn_norm_w, gdn_w_out, sg_w_in, sg_ln_g, sg_ln_b, sg_w_sp, sg_b_sp, sg_w_out, ffn_w_in, ffn_conv_w, ffn_w_out, ln_g, ln_b):
    bsz, s, d = x.shape
    depth = ffn_w_in.shape[0]
    alpha = float((2 * depth) ** 0.25)
    n_mixers = 3
    ffn_w_in_b = ffn_w_in.astype(BF16)
    ffn_w_out_b = ffn_w_out.astype(BF16)
    outs = []
    for bi in range(bsz):
        h = x.reshape(s, d) if bsz == 1 else x[bi]
        for i in range(depth):
            mixer, j = i % n_mixers, i // n_mixers
            if mixer == 0:
                h = _gla_layer(h, gla_w_in[j], gla_w_gk2[j], gla_b_gk[j], gla_norm_w[j], gla_w_out[j],
                               ln_g[i, 0], ln_b[i, 0], alpha=alpha)
            elif mixer == 1:
                h = _gdn_layer(h, gdn_w_in[j], gdn_conv_w[j], gdn_a_log[j], gdn_dt_bias[j],
                               gdn_norm_w[j], gdn_w_out[j], ln_g[i, 0], ln_b[i, 0], alpha=alpha)
            else:
                h = _sgu_layer(h, sg_w_in[j], sg_ln_g[j], sg_ln_b[j], sg_w_sp[j], sg_b_sp[j],
                               sg_w_out[j], ln_g[i, 0], ln_b[i, 0], alpha=alpha)
            h = _ffn_layer(h, ffn_w_in_b, ffn_conv_w[i], ffn_w_out_b, ln_g[i, 1], ln_b[i, 1],
                           layer=i, alpha=alpha)
        outs.append(h)
    return outs[0].reshape(1, s, d) if bsz == 1 else jnp.stack(outs, axis=0)
```

```python
import functools

import jax
import jax.numpy as jnp
from jax import lax
from jax.experimental import pallas as pl
from jax.experimental.pallas import tpu as pltpu

F32 = jnp.float32
BF16 = jnp.bfloat16

LN_EPS = 1e-5
RMS_EPS = 1e-6
GLA_NORMALIZER = 16.0
CHUNK = 64
BLOCK = 256
SG_CHUNK = 128
SUBLANES = 8
LANES = 128
VMEM_LIMIT = 56 * 1024 * 1024


def _mm_general(a, b, dims):
    return lax.dot_general(a, b, (dims, ((), ())), preferred_element_type=F32)


def _mm(a, b):
    return _mm_general(a, b, ((1,), (0,)))


def _dot(a, b):
    return _mm(a.astype(BF16), b.astype(BF16))


def _dot_nt(a, b):
    return _mm_general(a.astype(BF16), b.astype(BF16), ((1,), (1,)))


def _dot_tn(a, b):
    return _mm_general(a.astype(BF16), b.astype(BF16), ((0,), (0,)))


def _dot_split(m_bf16, x):
    hi = x.astype(BF16)
    r1 = x - hi.astype(F32)
    mid = r1.astype(BF16)
    lo = (r1 - mid.astype(F32)).astype(BF16)
    return _mm(m_bf16, hi) + _mm(m_bf16, mid) + _mm(m_bf16, lo)


def _layer_norm(r, g, b):
    mu = jnp.mean(r, -1, keepdims=True)
    c = r - mu
    var = jnp.mean(c * c, -1, keepdims=True)
    return c * lax.rsqrt(var + LN_EPS) * g + b


def _gelu(x):
    return 0.5 * x * (1.0 + lax.erf(x * (2.0 ** -0.5)))


def _silu(x):
    h = 0.5 * x
    return h + h * jnp.tanh(h)


def _sigmoid(x):
    return 0.5 + 0.5 * jnp.tanh(0.5 * x)


def _softplus(x):
    return jnp.maximum(x, 0.0) + jnp.log(1.0 + jnp.exp(-jnp.abs(x)))


def _block_masks():
    row = lax.broadcasted_iota(jnp.int32, (BLOCK, BLOCK), 0)
    col = lax.broadcasted_iota(jnp.int32, (BLOCK, BLOCK), 1)
    same = (row // CHUNK) == (col // CHUNK)
    incl = same & (col <= row)
    strict = same & (col < row)
    return incl, strict


def _per_chunk_row(x, r):
    parts = []
    for c in range(BLOCK // CHUNK):
        parts.append(jnp.broadcast_to(x[c * CHUNK + r:c * CHUNK + r + 1, :], (CHUNK, x.shape[1])))
    return jnp.concatenate(parts, axis=0)


def _restride(stage_ref, slot, val, starts, stride):
    tiles = val.shape[1] // LANES
    for c in range(tiles):
        stage_ref[slot, c] = val[:, c * LANES:(c + 1) * LANES]
    rows = []
    for st in starts:
        rows.append(jnp.concatenate(
            [stage_ref[slot, c, pl.ds(st, SUBLANES, stride=stride), :] for c in range(tiles)], axis=1))
    return jnp.concatenate(rows, axis=0)


def _to_interleaved(stage_ref, slot, val):
    n = val.shape[0] // SUBLANES
    return _restride(stage_ref, slot, val, range(n), n)


def _from_interleaved(stage_ref, slot, val):
    n = val.shape[0] // SUBLANES
    starts = [SUBLANES * ((SUBLANES * j) % n) + (SUBLANES * j) // n for j in range(n)]
    return _restride(stage_ref, slot, val, starts, SUBLANES)


def _project_interleaved(buf_ref, slot, carry_ref, xb, w_ref, col0, width, ts, taps):
    hist = SUBLANES * (taps - 1)
    buf_ref[slot, hist:hist + ts, :] = _mm(xb, w_ref[:, col0:col0 + width])
    first = lax.broadcasted_iota(jnp.int32, (SUBLANES, width), 0) == 0
    for m in range(1, taps):
        at = hist - SUBLANES * m
        cur = buf_ref[slot, at + ts:at + ts + SUBLANES, :]
        prev = carry_ref[at:at + SUBLANES, col0:col0 + width]
        buf_ref[slot, at:at + SUBLANES, :] = jnp.where(
            first, pltpu.roll(prev, 1, axis=0), pltpu.roll(cur, 1, axis=0))
    carry_ref[:, col0:col0 + width] = buf_ref[slot, ts:ts + hist, :]


def _conv_interleaved(buf_ref, slot, cw, ts):
    out = cw[0:1, :] * buf_ref[slot, 0:ts, :]
    for t in range(1, cw.shape[0]):
        out = out + cw[t:t + 1, :] * buf_ref[slot, SUBLANES * t:SUBLANES * t + ts, :]
    return out


def _ffn_kernel(x_ref, wi_ref, cw_ref, wo_ref, g_ref, b_ref, o_ref, hg_s, hu_s, carry_s, act_s,
                stage_s, *, ts, tf, nf, alpha):
    @pl.when(pl.program_id(0) == 0)
    def _():
        carry_s[...] = jnp.zeros_like(carry_s)

    dff = nf * tf
    taps = cw_ref.shape[0]
    x = _to_interleaved(stage_s, 0, x_ref[...])
    xb = x.astype(BF16)
    for j in range(nf):
        slot = j % 2
        _project_interleaved(hg_s, slot, carry_s, xb, wi_ref, j * tf, tf, ts, taps)
        _project_interleaved(hu_s, slot, carry_s, xb, wi_ref, dff + j * tf, tf, ts, taps)
        gate = _conv_interleaved(hg_s, slot, cw_ref[:, j * tf:(j + 1) * tf], ts)
        up = _conv_interleaved(hu_s, slot, 0.5 * cw_ref[:, dff + j * tf:dff + (j + 1) * tf], ts)
        act = gate * (1.0 + lax.erf(gate * (2.0 ** -0.5))) * up
        act_s[:, j * tf:(j + 1) * tf] = act.astype(BF16)
    y = _mm(act_s[...], wo_ref[...])
    o_ref[...] = _from_interleaved(stage_s, 1, _layer_norm(alpha * x + y, g_ref[...], b_ref[...]))


def _resident(*shape):
    return pl.BlockSpec(shape, lambda i: (0,) * len(shape), pipeline_mode=pl.Buffered(1))


def _resident_layer(layer, *shape):
    return pl.BlockSpec((None,) + shape, lambda i: (layer,) + (0,) * len(shape),
                        pipeline_mode=pl.Buffered(1))


def _ffn_layer(h, w_in_all, conv_w, w_out_all, ln_g, ln_b, *, layer, alpha, ts=512, tf=256):
    s, d = h.shape
    dff = w_out_all.shape[1]
    nf = dff // tf
    n = s // ts
    assert s % ts == 0 and dff % tf == 0 and ts % (SUBLANES * SUBLANES) == 0
    hist = SUBLANES * (conv_w.shape[0] - 1)
    kern = functools.partial(_ffn_kernel, ts=ts, tf=tf, nf=nf, alpha=alpha)
    return pl.pallas_call(
        kern,
        grid=(n,),
        in_specs=[
            pl.BlockSpec((ts, d), lambda i: (i, 0)),
            _resident_layer(layer, d, 2 * dff), _resident(conv_w.shape[0], 2 * dff),
            _resident_layer(layer, dff, d),
            _resident(1, d), _resident(1, d),
        ],
        out_specs=pl.BlockSpec((ts, d), lambda i: (i, 0)),
        out_shape=jax.ShapeDtypeStruct((s, d), F32),
        scratch_shapes=[
            pltpu.VMEM((2, ts + hist, tf), F32),
            pltpu.VMEM((2, ts + hist, tf), F32),
            pltpu.VMEM((hist, 2 * dff), F32),
            pltpu.VMEM((ts, dff), BF16),
            pltpu.VMEM((2, d // LANES, ts, LANES), F32),
        ],
        compiler_params=pltpu.CompilerParams(
            dimension_semantics=("arbitrary",), vmem_limit_bytes=VMEM_LIMIT),
        name="conv_ffn",
    )(h, w_in_all, conv_w, w_out_all, ln_g.reshape(1, d), ln_b.reshape(1, d))


def _sgu_kernel(x_ref, wu_ref, wv_ref, lg_ref, lb_ref, wsp_ref, bsp_ref, wo_ref, g_ref, b_ref,
                o_ref, m_ref, *, ts, groups, alpha):
    x = x_ref[...]
    xb = x.astype(BF16)
    u = _gelu(_mm(xb, wu_ref[...]))
    v = _gelu(_mm(xb, wv_ref[...]))
    v = _layer_norm(v, lg_ref[...], lb_ref[...])
    gw = v.shape[1] // groups
    row = lax.broadcasted_iota(jnp.int32, (SG_CHUNK, SG_CHUNK), 0)
    col = lax.broadcasted_iota(jnp.int32, (SG_CHUNK, SG_CHUNK), 1)
    causal = col <= row
    bsp = bsp_ref[...]
    for g in range(groups):
        wg = jnp.where(causal, wsp_ref[g], 0.0).astype(BF16)
        bias = jnp.broadcast_to(bsp[:, g:g + 1], (SG_CHUNK, gw))
        for c in range(ts // SG_CHUNK):
            vv = v[c * SG_CHUNK:(c + 1) * SG_CHUNK, g * gw:(g + 1) * gw].astype(BF16)
            mixed = _mm(wg, vv) + bias
            m_ref[c * SG_CHUNK:(c + 1) * SG_CHUNK, g * gw:(g + 1) * gw] = (
                u[c * SG_CHUNK:(c + 1) * SG_CHUNK, g * gw:(g + 1) * gw] * mixed).astype(BF16)
    y = _mm(m_ref[...], wo_ref[...])
    o_ref[...] = _layer_norm(alpha * x + y, g_ref[...], b_ref[...])


def _sgu_layer(h, w_in, sg_ln_g, sg_ln_b, w_sp, b_sp, w_out, ln_g, ln_b, *, alpha, ts=512):
    s, d = h.shape
    width = w_out.shape[0]
    groups = w_sp.shape[0]
    assert s % ts == 0 and ts % SG_CHUNK == 0
    w_in_b = w_in.astype(BF16)
    full = _resident
    kern = functools.partial(_sgu_kernel, ts=ts, groups=groups, alpha=alpha)
    return pl.pallas_call(
        kern,
        grid=(s // ts,),
        in_specs=[
            pl.BlockSpec((ts, d), lambda i: (i, 0)),
            pl.BlockSpec((d, width), lambda i: (0, 0)),
            pl.BlockSpec((d, width), lambda i: (0, 1)),
            full(1, width), full(1, width),
            full(groups, SG_CHUNK, SG_CHUNK),
            full(SG_CHUNK, groups),
            full(width, d), full(1, d), full(1, d),
        ],
        out_specs=pl.BlockSpec((ts, d), lambda i: (i, 0)),
        out_shape=jax.ShapeDtypeStruct((s, d), F32),
        scratch_shapes=[pltpu.VMEM((ts, width), BF16)],
        compiler_params=pltpu.CompilerParams(
            dimension_semantics=("arbitrary",), vmem_limit_bytes=VMEM_LIMIT),
        name="sgu_mixer",
    )(h, w_in_b, w_in_b, sg_ln_g.reshape(1, width), sg_ln_b.reshape(1, width), w_sp, b_sp.T,
      w_out.astype(BF16), ln_g.reshape(1, d), ln_b.reshape(1, d))


def _gla_kernel(x_ref, w_ref, wlr_ref, wgk2_ref, bgk_ref, nw_ref, wo_ref,
                g_ref, b_ref, o_ref, s_ref, q_s, k_s, v_s, gk_s, gate_s, vb_s, oi_s, qd_s, kd_s, dec_s,
                og_s, *, ts, heads, dk, dv, alpha):
    i = pl.program_id(0)

    @pl.when(i == 0)
    def _():
        s_ref[...] = jnp.zeros_like(s_ref)

    x = x_ref[...]
    xb = x.astype(BF16)
    hk, hv = heads * dk, heads * dv
    q_s[...] = _mm(xb, w_ref[:, 0:hk]) * (dk ** -0.5)
    k_s[...] = _mm(xb, w_ref[:, hk:2 * hk])
    v_s[...] = _mm(xb, w_ref[:, 2 * hk:2 * hk + hv])
    lr = _mm(xb, wlr_ref[...])
    z = _dot(lr, wgk2_ref[...]) + bgk_ref[...]
    gk_s[...] = -_softplus(-z) * (1.0 / GLA_NORMALIZER)

    gate_s[...] = _mm(xb, w_ref[:, 2 * hk + hv:2 * hk + 2 * hv])

    incl, _ = _block_masks()
    tril = jnp.where(incl, 1.0, 0.0).astype(BF16)
    nchunk = BLOCK // CHUNK

    for blk in range(ts // BLOCK):
        r0 = blk * BLOCK
        bcum_all = _dot_split(tril, gk_s[r0:r0 + BLOCK, :])
        for h in range(heads):
            bc = bcum_all[:, h * dk:(h + 1) * dk]
            bref = _per_chunk_row(bc, CHUNK // 2)
            blast = _per_chunk_row(bc, CHUNK - 1)
            qh = q_s[r0:r0 + BLOCK, h * dk:(h + 1) * dk]
            kh = k_s[r0:r0 + BLOCK, h * dk:(h + 1) * dk]
            vh = v_s[r0:r0 + BLOCK, h * dv:(h + 1) * dv].astype(BF16)
            sl = blk * heads + h
            vb_s[sl] = vh
            scores = _dot_nt(qh * jnp.exp(bc - bref), kh * jnp.exp(bref - bc))
            oi_s[sl] = _dot(jnp.where(incl, scores, 0.0), vh)
            qd_s[sl] = (qh * jnp.exp(bc)).astype(BF16)
            kd_s[sl] = (kh * jnp.exp(blast - bc)).astype(BF16)
            dec_s[sl] = jnp.exp(bc.T)

        for c in range(nchunk):
            rows = slice(c * CHUNK, (c + 1) * CHUNK)
            last = c * CHUNK + CHUNK - 1
            for h in range(heads):
                sl = blk * heads + h
                state = s_ref[h]
                oi_s[sl, rows, :] += _mm(qd_s[sl, rows, :], state.astype(BF16))
                d_state = _dot_tn(kd_s[sl, rows, :], vb_s[sl, rows, :])
                s_ref[h] = state * dec_s[sl, :, last:last + 1] + d_state

        for h in range(heads):
            o = oi_s[blk * heads + h]
            o = o * lax.rsqrt(jnp.mean(o * o, -1, keepdims=True) + RMS_EPS) * nw_ref[...]
            gate = gate_s[r0:r0 + BLOCK, h * dv:(h + 1) * dv]
            og_s[r0:r0 + BLOCK, h * dv:(h + 1) * dv] = (o * _silu(gate)).astype(BF16)

    y = _mm(og_s[...], wo_ref[...])
    o_ref[...] = _layer_norm(alpha * x + y, g_ref[...], b_ref[...])


def _gla_layer(h, w_in, w_gk2, b_gk, norm_w, w_out, ln_g, ln_b, *, alpha, ts=512):
    s, d = h.shape
    dv = norm_w.shape[0]
    hv = w_out.shape[0]
    heads = hv // dv
    hk = w_gk2.shape[1]
    dk = hk // heads
    rank = w_gk2.shape[0]
    lr_pad = 128
    slots = (ts // BLOCK) * heads
    assert s % ts == 0 and ts % BLOCK == 0
    w_in_b = w_in.astype(BF16)
    wlr = jnp.pad(w_in_b[:, 2 * hk + 2 * hv:], ((0, 0), (0, lr_pad - rank)))
    wgk2 = jnp.pad(w_gk2.astype(BF16), ((0, lr_pad - rank), (0, 0)))
    full = _resident
    kern = functools.partial(_gla_kernel, ts=ts, heads=heads, dk=dk, dv=dv, alpha=alpha)
    return pl.pallas_call(
        kern,
        grid=(s // ts,),
        in_specs=[
            pl.BlockSpec((ts, d), lambda i: (i, 0)),
            full(d, w_in.shape[1]), full(d, lr_pad),
            full(lr_pad, hk), full(1, hk), full(1, dv), full(hv, d), full(1, d), full(1, d),
        ],
        out_specs=pl.BlockSpec((ts, d), lambda i: (i, 0)),
        out_shape=jax.ShapeDtypeStruct((s, d), F32),
        scratch_shapes=[
            pltpu.VMEM((heads, dk, dv), F32),
            pltpu.VMEM((ts, hk), F32),
            pltpu.VMEM((ts, hk), F32),
            pltpu.VMEM((ts, hv), F32),
            pltpu.VMEM((ts, hk), F32),
            pltpu.VMEM((ts, hv), F32),
            pltpu.VMEM((slots, BLOCK, dv), BF16),
            pltpu.VMEM((slots, BLOCK, dv), F32),
            pltpu.VMEM((slots, BLOCK, dk), BF16),
            pltpu.VMEM((slots, BLOCK, dk), BF16),
            pltpu.VMEM((slots, dk, BLOCK), F32),
            pltpu.VMEM((ts, hv), BF16),
        ],
        compiler_params=pltpu.CompilerParams(
            dimension_semantics=("arbitrary",), vmem_limit_bytes=VMEM_LIMIT),
        name="gla_mixer",
    )(h, w_in_b, wlr, wgk2, b_gk.reshape(1, hk), norm_w.reshape(1, dv),
      w_out.astype(BF16), ln_g.reshape(1, d), ln_b.reshape(1, d))


def _gdn_kernel(x_ref, w_ref, wb_ref, wa_ref, cw_ref, alog_ref, dtb_ref, nw_ref, wo_ref,
                g_ref, b_ref, o_ref, s_ref, carry_ref, pre_s, stage_s, qkv_s, gate_s, beta_s, glog_s, x_s, p_s,
                rhs_s, u_s, w_s, qk_s, qd_s, kd_s, vn_s, o_s, og_s, *, ts, heads, dk, dv, alpha):
    i = pl.program_id(0)
    hk = heads * dk

    @pl.when(i == 0)
    def _():
        s_ref[...] = jnp.zeros_like(s_ref)
        carry_ref[...] = jnp.zeros_like(carry_ref)

    x = x_ref[...]
    xb = x.astype(BF16)
    xib = _to_interleaved(stage_s, 0, x).astype(BF16)
    tc = pre_s.shape[2]
    taps = cw_ref.shape[0]
    for j in range(qkv_s.shape[1] // tc):
        slot = j % 2
        _project_interleaved(pre_s, slot, carry_ref, xib, w_ref, j * tc, tc, ts, taps)
        act = _silu(_conv_interleaved(pre_s, slot, cw_ref[:, j * tc:(j + 1) * tc], ts))
        qkv_s[:, j * tc:(j + 1) * tc] = _from_interleaved(stage_s, slot, act)

    conv_ch = qkv_s.shape[1]
    gate_s[...] = _mm(xb, w_ref[:, conv_ch:conv_ch + heads * dv])
    beta_s[...] = _sigmoid(_mm(xb, wb_ref[...]))
    a_lin = _mm(xb, wa_ref[...])
    glog_s[...] = -jnp.exp(alog_ref[...]) * _softplus(a_lin + dtb_ref[...])

    nchunk = BLOCK // CHUNK
    steps = (CHUNK - 1).bit_length() - 1

    def block_body(blk):
        rows_blk = pl.ds(blk * BLOCK, BLOCK)

        def hb(h):
            return blk * heads + h

        incl, strict = _block_masks()
        tril = jnp.where(incl, 1.0, 0.0).astype(BF16)
        row = lax.broadcasted_iota(jnp.int32, (BLOCK, BLOCK), 0)
        col = lax.broadcasted_iota(jnp.int32, (BLOCK, BLOCK), 1)
        bcum = _dot_split(tril, glog_s[rows_blk, :])
        bcum_t = bcum.T
        blast = _per_chunk_row(bcum, CHUNK - 1)
        e_b = jnp.exp(bcum)
        e_kd = jnp.exp(blast - bcum)
        beta_blk = beta_s[rows_blk, :]

        for h in range(heads):
            q = qkv_s[rows_blk, h * dk:(h + 1) * dk]
            k = qkv_s[rows_blk, hk + h * dk:hk + (h + 1) * dk]
            v = qkv_s[rows_blk, 2 * hk + h * dv:2 * hk + (h + 1) * dv]
            q = q * lax.rsqrt(jnp.sum(q * q, -1, keepdims=True) + RMS_EPS) * (dk ** -0.5)
            k = k * lax.rsqrt(jnp.sum(k * k, -1, keepdims=True) + RMS_EPS)
            beta = beta_blk[:, h:h + 1]
            diff = bcum[:, h:h + 1] - bcum_t[h:h + 1, :]
            decay = jnp.exp(jnp.where(incl, diff, -jnp.inf))
            kb = k * beta
            kbf = k.astype(BF16)
            a = jnp.where(strict, -(_dot_nt(kb, kbf) * decay), 0.0)
            p_s[hb(h)] = a.astype(BF16)
            x_s[hb(h)] = jnp.where(row == col, 1.0, a)
            rhs_s[hb(h), :, 0:dv] = (v * beta).astype(BF16)
            rhs_s[hb(h), :, dv:dv + dk] = (kb * e_b[:, h:h + 1]).astype(BF16)
            qk_s[hb(h)] = (_dot_nt(q, kbf) * decay).astype(BF16)
            qd_s[hb(h)] = (q * e_b[:, h:h + 1]).astype(BF16)
            kd_s[hb(h)] = (k * e_kd[:, h:h + 1]).astype(BF16)

        for step in range(steps):
            for h in range(heads):
                pb = _mm(p_s[hb(h)], p_s[hb(h)]).astype(BF16)
                if step + 1 < steps:
                    p_s[hb(h)] = pb
                xh = x_s[hb(h)]
                x_s[hb(h)] = xh + _mm(xh.astype(BF16), pb)

        for h in range(heads):
            sol = _mm(x_s[hb(h)].astype(BF16), rhs_s[hb(h)])
            u_s[hb(h)] = sol[:, :dv]
            w_s[hb(h)] = sol[:, dv:].astype(BF16)

        for c in range(nchunk):
            rows = slice(c * CHUNK, (c + 1) * CHUNK)
            last = c * CHUNK + CHUNK - 1
            for h in range(heads):
                state = s_ref[h]
                sb = state.astype(BF16)
                v_new = u_s[hb(h), rows, :] - _mm(w_s[hb(h), rows, :], sb)
                vb = v_new.astype(BF16)
                vn_s[hb(h), rows, :] = vb
                o_s[hb(h), rows, :] = _mm(qd_s[hb(h), rows, :], sb)
                cd = jnp.exp(blast[last:last + 1, h:h + 1])
                s_ref[h] = state * cd + _dot_tn(kd_s[hb(h), rows, :], vb)

        for h in range(heads):
            o = o_s[hb(h)] + _mm(qk_s[hb(h)], vn_s[hb(h)])
            o = o * lax.rsqrt(jnp.mean(o * o, -1, keepdims=True) + RMS_EPS) * nw_ref[...]
            gate = gate_s[rows_blk, h * dv:(h + 1) * dv]
            og_s[rows_blk, h * dv:(h + 1) * dv] = (o * _silu(gate)).astype(BF16)

    for blk in range(ts // BLOCK):
        block_body(blk)

    y = _mm(og_s[...], wo_ref[...])
    o_ref[...] = _layer_norm(alpha * x + y, g_ref[...], b_ref[...])


def _gdn_layer(h, w_in, conv_w, a_log, dt_bias, norm_w, w_out, ln_g, ln_b, *, alpha, ts=512):
    s, d = h.shape
    dv = norm_w.shape[0]
    hv = w_out.shape[0]
    heads = hv // dv
    conv_ch = conv_w.shape[1]
    hk = (conv_ch - hv) // 2
    dk = hk // heads
    pad = 128
    assert s % ts == 0 and ts % BLOCK == 0
    hist = SUBLANES * (conv_w.shape[0] - 1)
    slots = (ts // BLOCK) * heads
    w_in_b = w_in.astype(BF16)
    wb = jnp.pad(w_in_b[:, conv_ch + hv:conv_ch + hv + heads], ((0, 0), (0, pad - heads)))
    wa = jnp.pad(w_in_b[:, conv_ch + hv + heads:], ((0, 0), (0, pad - heads)))
    alog = jnp.pad(a_log.reshape(1, heads), ((0, 0), (0, pad - heads)))
    dtb = jnp.pad(dt_bias.reshape(1, heads), ((0, 0), (0, pad - heads)))
    full = _resident
    kern = functools.partial(_gdn_kernel, ts=ts, heads=heads, dk=dk, dv=dv, alpha=alpha)
    return pl.pallas_call(
        kern,
        grid=(s // ts,),
        in_specs=[
            pl.BlockSpec((ts, d), lambda i: (i, 0)),
            full(d, w_in.shape[1]), full(d, pad), full(d, pad),
            full(conv_w.shape[0], conv_ch), full(1, pad), full(1, pad), full(1, dv),
            full(hv, d), full(1, d), full(1, d),
        ],
        out_specs=pl.BlockSpec((ts, d), lambda i: (i, 0)),
        out_shape=jax.ShapeDtypeStruct((s, d), F32),
        scratch_shapes=[
            pltpu.VMEM((heads, dk, dv), F32),
            pltpu.VMEM((hist, conv_ch), F32),
            pltpu.VMEM((2, ts + hist, 512), F32),
            pltpu.VMEM((2, d // LANES, ts, LANES), F32),
            pltpu.VMEM((ts, conv_ch), F32),
            pltpu.VMEM((ts, hv), F32),
            pltpu.VMEM((ts, pad), F32),
            pltpu.VMEM((ts, pad), F32),
            pltpu.VMEM((slots, BLOCK, BLOCK), F32),
            pltpu.VMEM((slots, BLOCK, BLOCK), BF16),
            pltpu.VMEM((slots, BLOCK, dv + dk), BF16),
            pltpu.VMEM((slots, BLOCK, dv), F32),
            pltpu.VMEM((slots, BLOCK, dk), BF16),
            pltpu.VMEM((slots, BLOCK, BLOCK), BF16),
            pltpu.VMEM((slots, BLOCK, dk), BF16),
            pltpu.VMEM((slots, BLOCK, dk), BF16),
            pltpu.VMEM((slots, BLOCK, dv), BF16),
            pltpu.VMEM((slots, BLOCK, dv), F32),
            pltpu.VMEM((ts, hv), BF16),
        ],
        compiler_params=pltpu.CompilerParams(
            dimension_semantics=("arbitrary",), vmem_limit_bytes=VMEM_LIMIT),
        name="gdn_mixer",
    )(h, w_in_b, wb, wa, conv_w, alog, dtb, norm_w.reshape(1, dv), w_out.astype(BF16),
      ln_g.reshape(1, d), ln_b.reshape(1, d))


def kernel(x, gla_w_in, gla_w_gk2, gla_b_gk, gla_norm_w, gla_w_out, gdn_w_in, gdn_conv_w, gdn_a_log, gdn_dt_bias, gdn_norm_w, gdn_w_out, sg_w_in, sg_ln_g, sg_ln_b, sg_w_sp, sg_b_sp, sg_w_out, ffn_w_in, ffn_conv_w, ffn_w_out, ln_g, ln_b):
    bsz, s, d = x.shape
    depth = ffn_w_in.shape[0]
    alpha = float((2 * depth) ** 0.25)
    n_mixers = 3
    ffn_w_in_b = ffn_w_in.astype(BF16)
    ffn_w_out_b = ffn_w_out.astype(BF16)
    outs = []
    for bi in range(bsz):
        h = x.reshape(s, d) if bsz == 1 else x[bi]
        for i in range(depth):
            mixer, j = i % n_mixers, i // n_mixers
            if mixer == 0:
                h = _gla_layer(h, gla_w_in[j], gla_w_gk2[j], gla_b_gk[j], gla_norm_w[j], gla_w_out[j],
                               ln_g[i, 0], ln_b[i, 0], alpha=alpha)
            elif mixer == 1:
                h = _gdn_layer(h, gdn_w_in[j], gdn_conv_w[j], gdn_a_log[j], gdn_dt_bias[j],
                               gdn_norm_w[j], gdn_w_out[j], ln_g[i, 0], ln_b[i, 0], alpha=alpha)
            else:
                h = _sgu_layer(h, sg_w_in[j], sg_ln_g[j], sg_ln_b[j], sg_w_sp[j], sg_b_sp[j],
                               sg_w_out[j], ln_g[i, 0], ln_b[i, 0], alpha=alpha)
            h = _ffn_layer(h, ffn_w_in_b, ffn_conv_w[i], ffn_w_out_b, ln_g[i, 1], ln_b[i, 1],
                           layer=i, alpha=alpha)
        outs.append(h)
    return outs[0].reshape(1, s, d) if bsz == 1 else jnp.stack(outs, axis=0)
```

```python
import functools

import jax
import jax.numpy as jnp
from jax import lax
from jax.experimental import pallas as pl
from jax.experimental.pallas import tpu as pltpu

F32 = jnp.float32
BF16 = jnp.bfloat16

LN_EPS = 1e-5
RMS_EPS = 1e-6
GLA_NORMALIZER = 16.0
CHUNK = 64
BLOCK = 256
SG_CHUNK = 128
SUBLANES = 8
LANES = 128
VMEM_LIMIT = 56 * 1024 * 1024


def _mm_general(a, b, dims):
    return lax.dot_general(a, b, (dims, ((), ())), preferred_element_type=F32)


def _mm(a, b):
    return _mm_general(a, b, ((1,), (0,)))


def _dot(a, b):
    return _mm(a.astype(BF16), b.astype(BF16))


def _dot_nt(a, b):
    return _mm_general(a.astype(BF16), b.astype(BF16), ((1,), (1,)))


def _dot_tn(a, b):
    return _mm_general(a.astype(BF16), b.astype(BF16), ((0,), (0,)))


def _dot_split(m_bf16, x):
    hi = x.astype(BF16)
    r1 = x - hi.astype(F32)
    mid = r1.astype(BF16)
    lo = (r1 - mid.astype(F32)).astype(BF16)
    return _mm(m_bf16, hi) + _mm(m_bf16, mid) + _mm(m_bf16, lo)


def _layer_norm(r, g, b):
    mu = jnp.mean(r, -1, keepdims=True)
    c = r - mu
    var = jnp.mean(c * c, -1, keepdims=True)
    return c * lax.rsqrt(var + LN_EPS) * g + b


def _gelu(x):
    return 0.5 * x * (1.0 + lax.erf(x * (2.0 ** -0.5)))


def _silu(x):
    h = 0.5 * x
    return h + h * jnp.tanh(h)


def _sigmoid(x):
    return 0.5 + 0.5 * jnp.tanh(0.5 * x)


def _softplus(x):
    return jnp.maximum(x, 0.0) + jnp.log(1.0 + jnp.exp(-jnp.abs(x)))


def _block_masks():
    row = lax.broadcasted_iota(jnp.int32, (BLOCK, BLOCK), 0)
    col = lax.broadcasted_iota(jnp.int32, (BLOCK, BLOCK), 1)
    same = (row // CHUNK) == (col // CHUNK)
    incl = same & (col <= row)
    strict = same & (col < row)
    return incl, strict


def _per_chunk_row(x, r):
    parts = []
    for c in range(BLOCK // CHUNK):
        parts.append(jnp.broadcast_to(x[c * CHUNK + r:c * CHUNK + r + 1, :], (CHUNK, x.shape[1])))
    return jnp.concatenate(parts, axis=0)


def _restride(stage_ref, slot, val, starts, stride):
    tiles = val.shape[1] // LANES
    for c in range(tiles):
        stage_ref[slot, c] = val[:, c * LANES:(c + 1) * LANES]
    rows = []
    for st in starts:
        rows.append(jnp.concatenate(
            [stage_ref[slot, c, pl.ds(st, SUBLANES, stride=stride), :] for c in range(tiles)], axis=1))
    return jnp.concatenate(rows, axis=0)


def _to_interleaved(stage_ref, slot, val):
    n = val.shape[0] // SUBLANES
    return _restride(stage_ref, slot, val, range(n), n)


def _from_interleaved(stage_ref, slot, val):
    n = val.shape[0] // SUBLANES
    starts = [SUBLANES * ((SUBLANES * j) % n) + (SUBLANES * j) // n for j in range(n)]
    return _restride(stage_ref, slot, val, starts, SUBLANES)


def _project_interleaved(buf_ref, slot, carry_ref, xb, w_ref, col0, width, ts, taps):
    hist = SUBLANES * (taps - 1)
    buf_ref[slot, hist:hist + ts, :] = _mm(xb, w_ref[:, col0:col0 + width])
    first = lax.broadcasted_iota(jnp.int32, (SUBLANES, width), 0) == 0
    for m in range(1, taps):
        at = hist - SUBLANES * m
        cur = buf_ref[slot, at + ts:at + ts + SUBLANES, :]
        prev = carry_ref[at:at + SUBLANES, col0:col0 + width]
        buf_ref[slot, at:at + SUBLANES, :] = jnp.where(
            first, pltpu.roll(prev, 1, axis=0), pltpu.roll(cur, 1, axis=0))
    carry_ref[:, col0:col0 + width] = buf_ref[slot, ts:ts + hist, :]


def _conv_interleaved(buf_ref, slot, cw, ts):
    out = cw[0:1, :] * buf_ref[slot, 0:ts, :]
    for t in range(1, cw.shape[0]):
        out = out + cw[t:t + 1, :] * buf_ref[slot, SUBLANES * t:SUBLANES * t + ts, :]
    return out


def _load_interleaved(x_refs, ts):
    n = ts // SUBLANES
    return jnp.concatenate(
        [jnp.concatenate([r[pl.ds(k, SUBLANES, stride=n), :] for r in x_refs], axis=1)
         for k in range(n)], axis=0)


def _ffn_kernel(*refs, ts, tf, nf, alpha, lane_tiles):
    x_refs = refs[:lane_tiles]
    (wi_ref, cw_ref, wo_ref, g_ref, b_ref, o_ref, hg_s, hu_s, carry_s, act_s,
     stage_s) = refs[lane_tiles:]

    @pl.when(pl.program_id(0) == 0)
    def _():
        carry_s[...] = jnp.zeros_like(carry_s)

    dff = nf * tf
    taps = cw_ref.shape[0]
    x = _load_interleaved(x_refs, ts)
    xb = x.astype(BF16)
    for j in range(nf):
        slot = j % 2
        _project_interleaved(hg_s, slot, carry_s, xb, wi_ref, j * tf, tf, ts, taps)
        _project_interleaved(hu_s, slot, carry_s, xb, wi_ref, dff + j * tf, tf, ts, taps)
        gate = _conv_interleaved(hg_s, slot, cw_ref[:, j * tf:(j + 1) * tf], ts)
        up = _conv_interleaved(hu_s, slot, 0.5 * cw_ref[:, dff + j * tf:dff + (j + 1) * tf], ts)
        act = gate * (1.0 + lax.erf(gate * (2.0 ** -0.5))) * up
        act_s[:, j * tf:(j + 1) * tf] = act.astype(BF16)
    y = _mm(act_s[...], wo_ref[...])
    o_ref[...] = _from_interleaved(stage_s, 0, _layer_norm(alpha * x + y, g_ref[...], b_ref[...]))


def _resident(*shape):
    return pl.BlockSpec(shape, lambda i: (0,) * len(shape), pipeline_mode=pl.Buffered(1))


def _resident_layer(layer, *shape):
    return pl.BlockSpec((None,) + shape, lambda i: (layer,) + (0,) * len(shape),
                        pipeline_mode=pl.Buffered(1))


def _ffn_layer(h, w_in_all, conv_w, w_out_all, ln_g, ln_b, *, layer, alpha, ts=512, tf=256):
    s, d = h.shape
    dff = w_out_all.shape[1]
    nf = dff // tf
    n = s // ts
    assert s % ts == 0 and dff % tf == 0 and ts % (SUBLANES * SUBLANES) == 0
    hist = SUBLANES * (conv_w.shape[0] - 1)
    lane_tiles = d // LANES
    kern = functools.partial(_ffn_kernel, ts=ts, tf=tf, nf=nf, alpha=alpha, lane_tiles=lane_tiles)
    x_specs = [pl.BlockSpec((ts, LANES), lambda i, c=c: (i, c)) for c in range(lane_tiles)]
    return pl.pallas_call(
        kern,
        grid=(n,),
        in_specs=x_specs + [
            _resident_layer(layer, d, 2 * dff), _resident(conv_w.shape[0], 2 * dff),
            _resident_layer(layer, dff, d),
            _resident(1, d), _resident(1, d),
        ],
        out_specs=pl.BlockSpec((ts, d), lambda i: (i, 0)),
        out_shape=jax.ShapeDtypeStruct((s, d), F32),
        scratch_shapes=[
            pltpu.VMEM((2, ts + hist, tf), F32),
            pltpu.VMEM((2, ts + hist, tf), F32),
            pltpu.VMEM((hist, 2 * dff), F32),
            pltpu.VMEM((ts, dff), BF16),
            pltpu.VMEM((1, lane_tiles, ts, LANES), F32),
        ],
        compiler_params=pltpu.CompilerParams(
            dimension_semantics=("arbitrary",), vmem_limit_bytes=VMEM_LIMIT),
        name="conv_ffn",
    )(*([h] * lane_tiles), w_in_all, conv_w, w_out_all, ln_g.reshape(1, d), ln_b.reshape(1, d))


def _sgu_kernel(x_ref, wu_ref, wv_ref, lg_ref, lb_ref, wsp_ref, bsp_ref, wo_ref, g_ref, b_ref,
                o_ref, m_ref, *, ts, groups, alpha):
    x = x_ref[...]
    xb = x.astype(BF16)
    u = _gelu(_mm(xb, wu_ref[...]))
    v = _gelu(_mm(xb, wv_ref[...]))
    v = _layer_norm(v, lg_ref[...], lb_ref[...])
    gw = v.shape[1] // groups
    row = lax.broadcasted_iota(jnp.int32, (SG_CHUNK, SG_CHUNK), 0)
    col = lax.broadcasted_iota(jnp.int32, (SG_CHUNK, SG_CHUNK), 1)
    causal = col <= row
    bsp = bsp_ref[...]
    for g in range(groups):
        wg = jnp.where(causal, wsp_ref[g], 0.0).astype(BF16)
        bias = jnp.broadcast_to(bsp[:, g:g + 1], (SG_CHUNK, gw))
        for c in range(ts // SG_CHUNK):
            vv = v[c * SG_CHUNK:(c + 1) * SG_CHUNK, g * gw:(g + 1) * gw].astype(BF16)
            mixed = _mm(wg, vv) + bias
            m_ref[c * SG_CHUNK:(c + 1) * SG_CHUNK, g * gw:(g + 1) * gw] = (
                u[c * SG_CHUNK:(c + 1) * SG_CHUNK, g * gw:(g + 1) * gw] * mixed).astype(BF16)
    y = _mm(m_ref[...], wo_ref[...])
    o_ref[...] = _layer_norm(alpha * x + y, g_ref[...], b_ref[...])


def _sgu_layer(h, w_in, sg_ln_g, sg_ln_b, w_sp, b_sp, w_out, ln_g, ln_b, *, alpha, ts=512):
    s, d = h.shape
    width = w_out.shape[0]
    groups = w_sp.shape[0]
    assert s % ts == 0 and ts % SG_CHUNK == 0
    w_in_b = w_in.astype(BF16)
    full = _resident
    kern = functools.partial(_sgu_kernel, ts=ts, groups=groups, alpha=alpha)
    return pl.pallas_call(
        kern,
        grid=(s // ts,),
        in_specs=[
            pl.BlockSpec((ts, d), lambda i: (i, 0)),
            pl.BlockSpec((d, width), lambda i: (0, 0)),
            pl.BlockSpec((d, width), lambda i: (0, 1)),
            full(1, width), full(1, width),
            full(groups, SG_CHUNK, SG_CHUNK),
            full(SG_CHUNK, groups),
            full(width, d), full(1, d), full(1, d),
        ],
        out_specs=pl.BlockSpec((ts, d), lambda i: (i, 0)),
        out_shape=jax.ShapeDtypeStruct((s, d), F32),
        scratch_shapes=[pltpu.VMEM((ts, width), BF16)],
        compiler_params=pltpu.CompilerParams(
            dimension_semantics=("arbitrary",), vmem_limit_bytes=VMEM_LIMIT),
        name="sgu_mixer",
    )(h, w_in_b, w_in_b, sg_ln_g.reshape(1, width), sg_ln_b.reshape(1, width), w_sp, b_sp.T,
      w_out.astype(BF16), ln_g.reshape(1, d), ln_b.reshape(1, d))


def _gla_kernel(x_ref, w_ref, wlr_ref, wgk2_ref, bgk_ref, nw_ref, wo_ref,
                g_ref, b_ref, o_ref, s_ref, q_s, k_s, v_s, gk_s, gate_s, vb_s, oi_s, qd_s, kd_s, dec_s,
                og_s, *, ts, heads, dk, dv, alpha):
    i = pl.program_id(0)

    @pl.when(i == 0)
    def _():
        s_ref[...] = jnp.zeros_like(s_ref)

    x = x_ref[...]
    xb = x.astype(BF16)
    hk, hv = heads * dk, heads * dv
    q_s[...] = _mm(xb, w_ref[:, 0:hk]) * (dk ** -0.5)
    k_s[...] = _mm(xb, w_ref[:, hk:2 * hk])
    v_s[...] = _mm(xb, w_ref[:, 2 * hk:2 * hk + hv])
    lr = _mm(xb, wlr_ref[...])
    z = _dot(lr, wgk2_ref[...]) + bgk_ref[...]
    gk_s[...] = -_softplus(-z) * (1.0 / GLA_NORMALIZER)

    gate_s[...] = _mm(xb, w_ref[:, 2 * hk + hv:2 * hk + 2 * hv])

    incl, _ = _block_masks()
    tril = jnp.where(incl, 1.0, 0.0).astype(BF16)
    nchunk = BLOCK // CHUNK

    for blk in range(ts // BLOCK):
        r0 = blk * BLOCK
        bcum_all = _dot_split(tril, gk_s[r0:r0 + BLOCK, :])
        for h in range(heads):
            bc = bcum_all[:, h * dk:(h + 1) * dk]
            bref = _per_chunk_row(bc, CHUNK // 2)
            blast = _per_chunk_row(bc, CHUNK - 1)
            qh = q_s[r0:r0 + BLOCK, h * dk:(h + 1) * dk]
            kh = k_s[r0:r0 + BLOCK, h * dk:(h + 1) * dk]
            vh = v_s[r0:r0 + BLOCK, h * dv:(h + 1) * dv].astype(BF16)
            sl = blk * heads + h
            vb_s[sl] = vh
            scores = _dot_nt(qh * jnp.exp(bc - bref), kh * jnp.exp(bref - bc))
            oi_s[sl] = _dot(jnp.where(incl, scores, 0.0), vh)
            qd_s[sl] = (qh * jnp.exp(bc)).astype(BF16)
            kd_s[sl] = (kh * jnp.exp(blast - bc)).astype(BF16)
            dec_s[sl] = jnp.exp(bc.T)

        for c in range(nchunk):
            rows = slice(c * CHUNK, (c + 1) * CHUNK)
            last = c * CHUNK + CHUNK - 1
            for h in range(heads):
                sl = blk * heads + h
                state = s_ref[h]
                oi_s[sl, rows, :] += _mm(qd_s[sl, rows, :], state.astype(BF16))
                d_state = _dot_tn(kd_s[sl, rows, :], vb_s[sl, rows, :])
                s_ref[h] = state * dec_s[sl, :, last:last + 1] + d_state

        for h in range(heads):
            o = oi_s[blk * heads + h]
            o = o * lax.rsqrt(jnp.mean(o * o, -1, keepdims=True) + RMS_EPS) * nw_ref[...]
            gate = gate_s[r0:r0 + BLOCK, h * dv:(h + 1) * dv]
            og_s[r0:r0 + BLOCK, h * dv:(h + 1) * dv] = (o * _silu(gate)).astype(BF16)

    y = _mm(og_s[...], wo_ref[...])
    o_ref[...] = _layer_norm(alpha * x + y, g_ref[...], b_ref[...])


def _gla_layer(h, w_in, w_gk2, b_gk, norm_w, w_out, ln_g, ln_b, *, alpha, ts=512):
    s, d = h.shape
    dv = norm_w.shape[0]
    hv = w_out.shape[0]
    heads = hv // dv
    hk = w_gk2.shape[1]
    dk = hk // heads
    rank = w_gk2.shape[0]
    lr_pad = 128
    slots = (ts // BLOCK) * heads
    assert s % ts == 0 and ts % BLOCK == 0
    w_in_b = w_in.astype(BF16)
    wlr = jnp.pad(w_in_b[:, 2 * hk + 2 * hv:], ((0, 0), (0, lr_pad - rank)))
    wgk2 = jnp.pad(w_gk2.astype(BF16), ((0, lr_pad - rank), (0, 0)))
    full = _resident
    kern = functools.partial(_gla_kernel, ts=ts, heads=heads, dk=dk, dv=dv, alpha=alpha)
    return pl.pallas_call(
        kern,
        grid=(s // ts,),
        in_specs=[
            pl.BlockSpec((ts, d), lambda i: (i, 0)),
            full(d, w_in.shape[1]), full(d, lr_pad),
            full(lr_pad, hk), full(1, hk), full(1, dv), full(hv, d), full(1, d), full(1, d),
        ],
        out_specs=pl.BlockSpec((ts, d), lambda i: (i, 0)),
        out_shape=jax.ShapeDtypeStruct((s, d), F32),
        scratch_shapes=[
            pltpu.VMEM((heads, dk, dv), F32),
            pltpu.VMEM((ts, hk), F32),
            pltpu.VMEM((ts, hk), F32),
            pltpu.VMEM((ts, hv), F32),
            pltpu.VMEM((ts, hk), F32),
            pltpu.VMEM((ts, hv), F32),
            pltpu.VMEM((slots, BLOCK, dv), BF16),
            pltpu.VMEM((slots, BLOCK, dv), F32),
            pltpu.VMEM((slots, BLOCK, dk), BF16),
            pltpu.VMEM((slots, BLOCK, dk), BF16),
            pltpu.VMEM((slots, dk, BLOCK), F32),
            pltpu.VMEM((ts, hv), BF16),
        ],
        compiler_params=pltpu.CompilerParams(
            dimension_semantics=("arbitrary",), vmem_limit_bytes=VMEM_LIMIT),
        name="gla_mixer",
    )(h, w_in_b, wlr, wgk2, b_gk.reshape(1, hk), norm_w.reshape(1, dv),
      w_out.astype(BF16), ln_g.reshape(1, d), ln_b.reshape(1, d))


def _gdn_kernel(x_ref, w_ref, wb_ref, wa_ref, cw_ref, alog_ref, dtb_ref, nw_ref, wo_ref,
                g_ref, b_ref, o_ref, s_ref, carry_ref, pre_s, stage_s, qkv_s, gate_s, beta_s, glog_s, x_s, p_s,
                rhs_s, u_s, w_s, qk_s, qd_s, kd_s, vn_s, o_s, og_s, *, ts, heads, dk, dv, alpha):
    i = pl.program_id(0)
    hk = heads * dk

    @pl.when(i == 0)
    def _():
        s_ref[...] = jnp.zeros_like(s_ref)
        carry_ref[...] = jnp.zeros_like(carry_ref)

    x = x_ref[...]
    xb = x.astype(BF16)
    xib = _to_interleaved(stage_s, 0, x).astype(BF16)
    tc = pre_s.shape[2]
    taps = cw_ref.shape[0]
    for j in range(qkv_s.shape[1] // tc):
        slot = j % 2
        _project_interleaved(pre_s, slot, carry_ref, xib, w_ref, j * tc, tc, ts, taps)
        act = _silu(_conv_interleaved(pre_s, slot, cw_ref[:, j * tc:(j + 1) * tc], ts))
        qkv_s[:, j * tc:(j + 1) * tc] = _from_interleaved(stage_s, slot, act)

    conv_ch = qkv_s.shape[1]
    gate_s[...] = _mm(xb, w_ref[:, conv_ch:conv_ch + heads * dv])
    beta_s[...] = _sigmoid(_mm(xb, wb_ref[...]))
    a_lin = _mm(xb, wa_ref[...])
    glog_s[...] = -jnp.exp(alog_ref[...]) * _softplus(a_lin + dtb_ref[...])

    nchunk = BLOCK // CHUNK
    steps = (CHUNK - 1).bit_length() - 1

    def block_body(blk):
        rows_blk = pl.ds(blk * BLOCK, BLOCK)

        def hb(h):
            return blk * heads + h

        incl, strict = _block_masks()
        tril = jnp.where(incl, 1.0, 0.0).astype(BF16)
        row = lax.broadcasted_iota(jnp.int32, (BLOCK, BLOCK), 0)
        col = lax.broadcasted_iota(jnp.int32, (BLOCK, BLOCK), 1)
        bcum = _dot_split(tril, glog_s[rows_blk, :])
        bcum_t = bcum.T
        blast = _per_chunk_row(bcum, CHUNK - 1)
        e_b = jnp.exp(bcum)
        e_kd = jnp.exp(blast - bcum)
        beta_blk = beta_s[rows_blk, :]

        for h in range(heads):
            q = qkv_s[rows_blk, h * dk:(h + 1) * dk]
            k = qkv_s[rows_blk, hk + h * dk:hk + (h + 1) * dk]
            v = qkv_s[rows_blk, 2 * hk + h * dv:2 * hk + (h + 1) * dv]
            q = q * lax.rsqrt(jnp.sum(q * q, -1, keepdims=True) + RMS_EPS) * (dk ** -0.5)
            k = k * lax.rsqrt(jnp.sum(k * k, -1, keepdims=True) + RMS_EPS)
            beta = beta_blk[:, h:h + 1]
            diff = bcum[:, h:h + 1] - bcum_t[h:h + 1, :]
            decay = jnp.exp(jnp.where(incl, diff, -jnp.inf))
            kb = k * beta
            kbf = k.astype(BF16)
            a = jnp.where(strict, -(_dot_nt(kb, kbf) * decay), 0.0)
            p_s[hb(h)] = a.astype(BF16)
            x_s[hb(h)] = jnp.where(row == col, 1.0, a)
            rhs_s[hb(h), :, 0:dv] = (v * beta).astype(BF16)
            rhs_s[hb(h), :, dv:dv + dk] = (kb * e_b[:, h:h + 1]).astype(BF16)
            qk_s[hb(h)] = (_dot_nt(q, kbf) * decay).astype(BF16)
            qd_s[hb(h)] = (q * e_b[:, h:h + 1]).astype(BF16)
            kd_s[hb(h)] = (k * e_kd[:, h:h + 1]).astype(BF16)

        for step in range(steps):
            for h in range(heads):
                pb = _mm(p_s[hb(h)], p_s[hb(h)]).astype(BF16)
                if step + 1 < steps:
                    p_s[hb(h)] = pb
                xh = x_s[hb(h)]
                x_s[hb(h)] = xh + _mm(xh.astype(BF16), pb)

        for h in range(heads):
            sol = _mm(x_s[hb(h)].astype(BF16), rhs_s[hb(h)])
            u_s[hb(h)] = sol[:, :dv]
            w_s[hb(h)] = sol[:, dv:].astype(BF16)

        for c in range(nchunk):
            rows = slice(c * CHUNK, (c + 1) * CHUNK)
            last = c * CHUNK + CHUNK - 1
            for h in range(heads):
                state = s_ref[h]
                sb = state.astype(BF16)
                v_new = u_s[hb(h), rows, :] - _mm(w_s[hb(h), rows, :], sb)
                vb = v_new.astype(BF16)
                vn_s[hb(h), rows, :] = vb
                o_s[hb(h), rows, :] = _mm(qd_s[hb(h), rows, :], sb)
                cd = jnp.exp(blast[last:last + 1, h:h + 1])
                s_ref[h] = state * cd + _dot_tn(kd_s[hb(h), rows, :], vb)

        for h in range(heads):
            o = o_s[hb(h)] + _mm(qk_s[hb(h)], vn_s[hb(h)])
            o = o * lax.rsqrt(jnp.mean(o * o, -1, keepdims=True) + RMS_EPS) * nw_ref[...]
            gate = gate_s[rows_blk, h * dv:(h + 1) * dv]
            og_s[rows_blk, h * dv:(h + 1) * dv] = (o * _silu(gate)).astype(BF16)

    for blk in range(ts // BLOCK):
        block_body(blk)

    y = _mm(og_s[...], wo_ref[...])
    o_ref[...] = _layer_norm(alpha * x + y, g_ref[...], b_ref[...])


def _gdn_layer(h, w_in, conv_w, a_log, dt_bias, norm_w, w_out, ln_g, ln_b, *, alpha, ts=512):
    s, d = h.shape
    dv = norm_w.shape[0]
    hv = w_out.shape[0]
    heads = hv // dv
    conv_ch = conv_w.shape[1]
    hk = (conv_ch - hv) // 2
    dk = hk // heads
    pad = 128
    assert s % ts == 0 and ts % BLOCK == 0
    hist = SUBLANES * (conv_w.shape[0] - 1)
    slots = (ts // BLOCK) * heads
    w_in_b = w_in.astype(BF16)
    wb = jnp.pad(w_in_b[:, conv_ch + hv:conv_ch + hv + heads], ((0, 0), (0, pad - heads)))
    wa = jnp.pad(w_in_b[:, conv_ch + hv + heads:], ((0, 0), (0, pad - heads)))
    alog = jnp.pad(a_log.reshape(1, heads), ((0, 0), (0, pad - heads)))
    dtb = jnp.pad(dt_bias.reshape(1, heads), ((0, 0), (0, pad - heads)))
    full = _resident
    kern = functools.partial(_gdn_kernel, ts=ts, heads=heads, dk=dk, dv=dv, alpha=alpha)
    return pl.pallas_call(
        kern,
        grid=(s // ts,),
        in_specs=[
            pl.BlockSpec((ts, d), lambda i: (i, 0)),
            full(d, w_in.shape[1]), full(d, pad), full(d, pad),
            full(conv_w.shape[0], conv_ch), full(1, pad), full(1, pad), full(1, dv),
            full(hv, d), full(1, d), full(1, d),
        ],
        out_specs=pl.BlockSpec((ts, d), lambda i: (i, 0)),
        out_shape=jax.ShapeDtypeStruct((s, d), F32),
        scratch_shapes=[
            pltpu.VMEM((heads, dk, dv), F32),
            pltpu.VMEM((hist, conv_ch), F32),
            pltpu.VMEM((2, ts + hist, 512), F32),
            pltpu.VMEM((2, d // LANES, ts, LANES), F32),
            pltpu.VMEM((ts, conv_ch), F32),
            pltpu.VMEM((ts, hv), F32),
            pltpu.VMEM((ts, pad), F32),
            pltpu.VMEM((ts, pad), F32),
            pltpu.VMEM((slots, BLOCK, BLOCK), F32),
            pltpu.VMEM((slots, BLOCK, BLOCK), BF16),
            pltpu.VMEM((slots, BLOCK, dv + dk), BF16),
            pltpu.VMEM((slots, BLOCK, dv), F32),
            pltpu.VMEM((slots, BLOCK, dk), BF16),
            pltpu.VMEM((slots, BLOCK, BLOCK), BF16),
            pltpu.VMEM((slots, BLOCK, dk), BF16),
            pltpu.VMEM((slots, BLOCK, dk), BF16),
            pltpu.VMEM((slots, BLOCK, dv), BF16),
            pltpu.VMEM((slots, BLOCK, dv), F32),
            pltpu.VMEM((ts, hv), BF16),
        ],
        compiler_params=pltpu.CompilerParams(
            dimension_semantics=("arbitrary",), vmem_limit_bytes=VMEM_LIMIT),
        name="gdn_mixer",
    )(h, w_in_b, wb, wa, conv_w, alog, dtb, norm_w.reshape(1, dv), w_out.astype(BF16),
      ln_g.reshape(1, d), ln_b.reshape(1, d))


def kernel(x, gla_w_in, gla_w_gk2, gla_b_gk, gla_norm_w, gla_w_out, gdn_w_in, gdn_conv_w, gdn_a_log, gdn_dt_bias, gdn_norm_w, gdn_w_out, sg_w_in, sg_ln_g, sg_ln_b, sg_w_sp, sg_b_sp, sg_w_out, ffn_w_in, ffn_conv_w, ffn_w_out, ln_g, ln_b):
    bsz, s, d = x.shape
    depth = ffn_w_in.shape[0]
    alpha = float((2 * depth) ** 0.25)
    n_mixers = 3
    ffn_w_in_b = ffn_w_in.astype(BF16)
    ffn_w_out_b = ffn_w_out.astype(BF16)
    outs = []
    for bi in range(bsz):
        h = x.reshape(s, d) if bsz == 1 else x[bi]
        for i in range(depth):
            mixer, j = i % n_mixers, i // n_mixers
            if mixer == 0:
                h = _gla_layer(h, gla_w_in[j], gla_w_gk2[j], gla_b_gk[j], gla_norm_w[j], gla_w_out[j],
                               ln_g[i, 0], ln_b[i, 0], alpha=alpha)
            elif mixer == 1:
                h = _gdn_layer(h, gdn_w_in[j], gdn_conv_w[j], gdn_a_log[j], gdn_dt_bias[j],
                               gdn_norm_w[j], gdn_w_out[j], ln_g[i, 0], ln_b[i, 0], alpha=alpha)
            else:
                h = _sgu_layer(h, sg_w_in[j], sg_ln_g[j], sg_ln_b[j], sg_w_sp[j], sg_b_sp[j],
                               sg_w_out[j], ln_g[i, 0], ln_b[i, 0], alpha=alpha)
            h = _ffn_layer(h, ffn_w_in_b, ffn_conv_w[i], ffn_w_out_b, ln_g[i, 1], ln_b[i, 1],
                           layer=i, alpha=alpha)
        outs.append(h)
    return outs[0].reshape(1, s, d) if bsz == 1 else jnp.stack(outs, axis=0)
```
